```python
import math
import jax, jax.numpy as jnp
from jax import lax
import numpy as np

D_MODEL = 4096
BATCH = 4
SEQ = 4096
DEPTH = 4

D_MIX = D_MODEL
M_HEADS = 4
M_DQK = D_MODEL // 16
M_DV = D_MODEL // 8
M_WIDTH = M_HEADS * M_DV
M_CHUNK = 64
A_HEADS = 4
A_DH = D_MODEL // 32
A_DV = 2 * A_DH
A_WIDTH = A_HEADS * A_DV
A_BLOCK = 128
C_WIDTH = D_MIX - M_WIDTH - A_WIDTH
C_KW = 3
D_FF = 4 * D_MODEL
ALPHA = (2 * DEPTH) ** 0.25
BETA = (8 * DEPTH) ** -0.25
LN_EPS = 1e-5
SUBLN_EPS = 1e-5

IN_LAYOUT = (
    ('m_q', M_HEADS * M_DQK), ('m_k', M_HEADS * M_DQK), ('m_v', M_WIDTH), ('m_o', M_WIDTH),
    ('m_i', M_HEADS), ('m_f', M_HEADS),
    ('a_q', A_HEADS * 2 * A_DH), ('a_k', A_HEADS * 2 * A_DH), ('a_v', A_WIDTH),
    ('c_b', C_WIDTH), ('c_c', C_WIDTH), ('c_x', C_WIDTH),
)
N_IN = sum(n for _, n in IN_LAYOUT)
VALUE_GROUPS = ('m_v', 'a_v', 'c_x')

kernel_name = 'hybrid_mlstm_diffattn_shortconv_deepnorm'


def layer_norm(x, g, b):
    xf = x.astype(jnp.float32)
    mu = xf.mean(-1, keepdims=True)
    var = jnp.square(xf - mu).mean(-1, keepdims=True)
    return ((xf - mu) * lax.rsqrt(var + LN_EPS) * g + b).astype(x.dtype)


def mlstm(q, k, v, i_pre, f_pre):
    f32 = jnp.float32
    B_, S_, H, dqk = q.shape
    dv = v.shape[-1]
    L = M_CHUNK
    nc = S_ // L

    def to_chunks(a):
        a = a.astype(f32).reshape((B_, nc, L, H) + a.shape[3:])
        return jnp.moveaxis(a, (1, 3), (0, 2))

    qc = to_chunks(q)
    kc = to_chunks(k) * (dqk ** -0.5)
    vc = to_chunks(v)
    ic = to_chunks(i_pre)
    lfc = to_chunks(jax.nn.log_sigmoid(f_pre.astype(f32)))
    causal = jnp.tril(jnp.ones((L, L), dtype=bool))

    def step(carry, xs):
        C, n, m = carry
        qj, kj, vj, ij, lfj = xs
        b = jnp.cumsum(lfj, axis=-1)
        dmat = jnp.where(causal, b[..., :, None] - b[..., None, :] + ij[..., None, :], -jnp.inf)
        inter = b + m[..., None]
        m_row = jnp.maximum(inter, dmat.max(-1))
        w = jnp.einsum('bhjd,bhsd->bhjs', qj, kj) * jnp.exp(dmat - m_row[..., None])
        g = jnp.exp(inter - m_row)
        num = jnp.einsum('bhjs,bhsv->bhjv', w, vj) + g[..., None] * jnp.einsum('bhjd,bhdv->bhjv', qj, C)
        den = w.sum(-1) + g * jnp.einsum('bhjd,bhd->bhj', qj, n)
        h = num / jnp.maximum(jnp.abs(den), jnp.exp(-m_row))[..., None]
        m_new = m_row[..., -1]
        wl = jnp.exp(b[..., -1:] - b + ij - m_new[..., None])
        gl = jnp.exp(b[..., -1] + m - m_new)
        C_new = gl[..., None, None] * C + jnp.einsum('bhs,bhsd,bhsv->bhdv', wl, kj, vj)
        n_new = gl[..., None] * n + jnp.einsum('bhs,bhsd->bhd', wl, kj)
        return (C_new, n_new, m_new), h

    init = (jnp.zeros((B_, H, dqk, dv), f32), jnp.zeros((B_, H, dqk), f32), jnp.zeros((B_, H), f32))
    _, h = lax.scan(step, init, (qc, kc, vc, ic, lfc))
    h = jnp.moveaxis(h, (0, 2), (1, 3))
    return h.reshape(B_, S_, H, dv)


def diff_attention(q, k, v, lam, lam_init, subln_g):
    f32 = jnp.float32
    B_, S_, H, _, dh = q.shape
    dv = v.shape[-1]
    nb = S_ // A_BLOCK
    slopes = 2.0 ** (-8.0 * jnp.arange(1, H + 1, dtype=f32) / H)
    kpos = jnp.arange(S_)
    scale = dh ** -0.5
    qb = jnp.moveaxis(q.reshape(B_, nb, A_BLOCK, H, 2, dh), 1, 0)
    starts = jnp.arange(nb) * A_BLOCK

    def block(args):
        qblk, start = args
        qpos = start + jnp.arange(A_BLOCK)
        dist = (qpos[:, None] - kpos[None, :]).astype(f32)
        bias = jnp.where(dist >= 0, -slopes[:, None, None] * dist, -jnp.inf)
        s = jnp.einsum('bqhmd,bshmd->bhmqs', qblk, k).astype(f32) * scale + bias[None, :, None]
        p = jax.nn.softmax(s, axis=-1)
        a = p[:, :, 0] - lam * p[:, :, 1]
        return jnp.einsum('bhqs,bshv->bqhv', a.astype(v.dtype), v)

    o = lax.map(block, (qb, starts))
    o = jnp.moveaxis(o, 0, 1).reshape(B_, S_, H, dv).astype(f32)
    o = o * lax.rsqrt(jnp.mean(jnp.square(o), -1, keepdims=True) + SUBLN_EPS) * subln_g
    return o * (1.0 - lam_init)


def short_gated_conv(b_gate, c_gate, u, w):
    S_ = u.shape[1]
    z = c_gate * u
    zp = jnp.pad(z, ((0, 0), (C_KW - 1, 0), (0, 0)))
    y = w[0] * zp[:, 0:S_]
    for j in range(1, C_KW):
        y = y + w[j] * zp[:, j:j + S_]
    return b_gate * y


def setup_inputs(seed: int = 0) -> dict:
    key = jax.random.key(seed)
    ks = jax.random.split(key, 20)
    f32 = jnp.float32
    x = jax.random.normal(ks[0], (BATCH, SEQ, D_MODEL), f32)
    col_scale = np.concatenate([np.full((n,), BETA if name in VALUE_GROUPS else 1.0, np.float32)
                                for name, n in IN_LAYOUT])
    w_in = jax.random.normal(ks[1], (DEPTH, D_MODEL, N_IN), f32) * (D_MODEL ** -0.5) * jnp.asarray(col_scale)
    m_b_i = 0.1 * jax.random.normal(ks[2], (DEPTH, M_HEADS), f32)
    m_b_f = 3.0 + 0.5 * jax.random.normal(ks[3], (DEPTH, M_HEADS), f32)
    a_lq1 = 0.1 * jax.random.normal(ks[4], (DEPTH, A_DH), f32)
    a_lk1 = 0.1 * jax.random.normal(ks[5], (DEPTH, A_DH), f32)
    a_lq2 = 0.1 * jax.random.normal(ks[6], (DEPTH, A_DH), f32)
    a_lk2 = 0.1 * jax.random.normal(ks[7], (DEPTH, A_DH), f32)
    a_subln_g = 1.0 + 0.02 * jax.random.normal(ks[8], (DEPTH, A_DV), f32)
    c_conv_w = jax.random.normal(ks[9], (DEPTH, C_KW, C_WIDTH), f32) * (C_KW ** -0.5)
    w_out = jax.random.normal(ks[10], (DEPTH, D_MIX, D_MODEL), f32) * (D_MIX ** -0.5) * BETA
    ln1_g = 1.0 + 0.02 * jax.random.normal(ks[11], (DEPTH, D_MODEL), f32)
    ln1_b = 0.02 * jax.random.normal(ks[12], (DEPTH, D_MODEL), f32)
    w_up = jax.random.normal(ks[13], (DEPTH, D_MODEL, D_FF), f32) * (D_MODEL ** -0.5) * BETA
    w_down = jax.random.normal(ks[14], (DEPTH, D_FF, D_MODEL), f32) * (D_FF ** -0.5) * BETA
    ln2_g = 1.0 + 0.02 * jax.random.normal(ks[15], (DEPTH, D_MODEL), f32)
    ln2_b = 0.02 * jax.random.normal(ks[16], (DEPTH, D_MODEL), f32)
    return {'x': x, 'w_in': w_in, 'm_b_i': m_b_i, 'm_b_f': m_b_f,
            'a_lq1': a_lq1, 'a_lk1': a_lk1, 'a_lq2': a_lq2, 'a_lk2': a_lk2, 'a_subln_g': a_subln_g,
            'c_conv_w': c_conv_w, 'w_out': w_out, 'ln1_g': ln1_g, 'ln1_b': ln1_b,
            'w_up': w_up, 'w_down': w_down, 'ln2_g': ln2_g, 'ln2_b': ln2_b}


def reference(x, w_in, m_b_i, m_b_f, a_lq1, a_lk1, a_lq2, a_lk2, a_subln_g,
              c_conv_w, w_out, ln1_g, ln1_b, w_up, w_down, ln2_g, ln2_b):
    B_, S_, _ = x.shape
    offsets = np.cumsum([n for _, n in IN_LAYOUT])[:-1].tolist()
    for l in range(DEPTH):
        proj = jnp.einsum('bsd,dn->bsn', x, w_in[l])
        (m_q, m_k, m_v, m_o, m_i, m_f, a_q, a_k, a_v, c_b, c_c, c_x) = jnp.split(proj, offsets, axis=-1)
        h_m = mlstm(m_q.reshape(B_, S_, M_HEADS, M_DQK), m_k.reshape(B_, S_, M_HEADS, M_DQK),
                    m_v.reshape(B_, S_, M_HEADS, M_DV),
                    m_i.astype(jnp.float32) + m_b_i[l], m_f.astype(jnp.float32) + m_b_f[l])
        y_m = (jax.nn.sigmoid(m_o.astype(jnp.float32)) * h_m.reshape(B_, S_, M_WIDTH)).astype(x.dtype)
        lam_init = 0.8 - 0.6 * math.exp(-0.3 * l)
        lam = (jnp.exp(jnp.sum(a_lq1[l] * a_lk1[l]).astype(jnp.float32))
               - jnp.exp(jnp.sum(a_lq2[l] * a_lk2[l]).astype(jnp.float32)) + lam_init)
        h_a = diff_attention(a_q.reshape(B_, S_, A_HEADS, 2, A_DH), a_k.reshape(B_, S_, A_HEADS, 2, A_DH),
                             a_v.reshape(B_, S_, A_HEADS, A_DV), lam, lam_init, a_subln_g[l])
        y_a = h_a.reshape(B_, S_, A_WIDTH).astype(x.dtype)
        y_c = short_gated_conv(c_b, c_c, c_x, c_conv_w[l])
        mix = jnp.einsum('bsm,md->bsd', jnp.concatenate([y_m, y_a, y_c], axis=-1), w_out[l])
        x = layer_norm(ALPHA * x + mix, ln1_g[l], ln1_b[l])
        hid = jnp.square(jax.nn.relu(jnp.einsum('bsd,df->bsf', x, w_up[l])))
        ff = jnp.einsum('bsf,fd->bsd', hid, w_down[l])
        x = layer_norm(ALPHA * x + ff, ln2_g[l], ln2_b[l])
    return x
```

```python
import functools
import math

import jax
import jax.numpy as jnp
from jax import lax
from jax.experimental import pallas as pl
from jax.experimental.pallas import tpu as pltpu

M_HEADS = 4
A_HEADS = 4
C_KW = 3
N_GATES = 2 * M_HEADS
LN_EPS = 1e-5
SUBLN_EPS = 1e-5

V7X_VMEM_LIMIT_BYTES = 56 * 1024 * 1024
LANES = 128
NEG_BIG = -1e30

F32 = jnp.float32
BF16 = jnp.bfloat16


def _tile(n, want):
    t = min(n, want)
    while n % t:
        t //= 2
    return t


def _params(*sem):
    return pltpu.CompilerParams(dimension_semantics=sem, vmem_limit_bytes=V7X_VMEM_LIMIT_BYTES)


def _mm_kernel(x_ref, w_ref, o_ref, *, relu2):
    acc = jnp.dot(x_ref[...], w_ref[...], preferred_element_type=F32)
    if relu2:
        acc = jnp.square(jnp.maximum(acc, 0.0))
    o_ref[...] = acc.astype(o_ref.dtype)


def _matmul(x, w, out_dtype, *, relu2=False, tm=1024, tn=1024):
    m, k = x.shape
    n = w.shape[1]
    tm, tn = _tile(m, tm), _tile(n, tn)
    return pl.pallas_call(
        functools.partial(_mm_kernel, relu2=relu2),
        grid=(m // tm, n // tn),
        in_specs=[pl.BlockSpec((tm, k), lambda i, j: (i, 0)),
                  pl.BlockSpec((k, tn), lambda i, j: (0, j))],
        out_specs=pl.BlockSpec((tm, tn), lambda i, j: (i, j)),
        out_shape=jax.ShapeDtypeStruct((m, n), out_dtype),
        compiler_params=_params("parallel", "parallel"),
    )(x, w)


def _mm_acc_kernel(x_ref, w_ref, o_ref, acc_ref):
    kk = pl.program_id(2)

    @pl.when(kk == 0)
    def _():
        acc_ref[...] = jnp.zeros_like(acc_ref)

    acc_ref[...] += jnp.dot(x_ref[...], w_ref[...], preferred_element_type=F32)

    @pl.when(kk == pl.num_programs(2) - 1)
    def _():
        o_ref[...] = acc_ref[...].astype(o_ref.dtype)


def _matmul_ktiled(x, w, out_dtype, *, tm=1024, tn=1024, tk=2048):
    m, k = x.shape
    n = w.shape[1]
    tm, tn, tk = _tile(m, tm), _tile(n, tn), _tile(k, tk)
    return pl.pallas_call(
        _mm_acc_kernel,
        grid=(m // tm, n // tn, k // tk),
        in_specs=[pl.BlockSpec((tm, tk), lambda i, j, kk: (i, kk)),
                  pl.BlockSpec((tk, tn), lambda i, j, kk: (kk, j))],
        out_specs=pl.BlockSpec((tm, tn), lambda i, j, kk: (i, j)),
        out_shape=jax.ShapeDtypeStruct((m, n), out_dtype),
        scratch_shapes=[pltpu.VMEM((tm, tn), F32)],
        compiler_params=_params("parallel", "parallel", "arbitrary"),
    )(x, w)


def _res_ln_kernel(x_ref, y_ref, g_ref, b_ref, o_ref, ob_ref, *, alpha):
    r = alpha * x_ref[...] + y_ref[...]
    mu = jnp.mean(r, axis=-1, keepdims=True)
    d = r - mu
    var = jnp.mean(d * d, axis=-1, keepdims=True)
    o = d * lax.rsqrt(var + LN_EPS) * g_ref[...] + b_ref[...]
    o_ref[...] = o
    ob_ref[...] = o.astype(BF16)


def _res_ln(x, y, g, b, alpha, *, tr=256):
    t, d = x.shape
    tr = _tile(t, tr)
    row = pl.BlockSpec((tr, d), lambda i: (i, 0))
    vec = pl.BlockSpec((1, d), lambda i: (0, 0))
    return pl.pallas_call(
        functools.partial(_res_ln_kernel, alpha=alpha),
        grid=(t // tr,),
        in_specs=[row, row, vec, vec],
        out_specs=[row, row],
        out_shape=[jax.ShapeDtypeStruct((t, d), F32), jax.ShapeDtypeStruct((t, d), BF16)],
        compiler_params=_params("parallel"),
    )(x, y, g.reshape(1, d), b.reshape(1, d))


def _mlstm_kernel(bias_ref, q_ref, k_ref, v_ref, og_ref, gi_ref, gf_ref, y_ref,
                  c_ref, n_ref, m_ref, *, scale):
    head = pl.program_id(1)

    @pl.when(pl.program_id(2) == 0)
    def _():
        c_ref[...] = jnp.zeros_like(c_ref)
        n_ref[...] = jnp.zeros_like(n_ref)
        m_ref[...] = jnp.zeros_like(m_ref)

    q = q_ref[...]
    k = k_ref[...]
    v = v_ref[...]
    ln = q.shape[0]
    i_row = gi_ref[...] + bias_ref[0, head]
    f_row = gf_ref[...] + bias_ref[1, head]
    lf_row = jnp.minimum(f_row, 0.0) - jnp.log(1.0 + jnp.exp(-jnp.abs(f_row)))

    rows = lax.broadcasted_iota(jnp.int32, (ln, ln), 0)
    cols = lax.broadcasted_iota(jnp.int32, (ln, ln), 1)
    causal = cols <= rows
    eye = cols == rows

    def to_col(row):
        return jnp.sum(jnp.where(eye, jnp.broadcast_to(row, (ln, ln)), 0.0), axis=1, keepdims=True)

    def to_row(col):
        return jnp.sum(jnp.where(eye, jnp.broadcast_to(col, (ln, ln)), 0.0), axis=0, keepdims=True)

    b_col = jnp.sum(jnp.where(causal, jnp.broadcast_to(lf_row, (ln, ln)), 0.0), axis=1, keepdims=True)
    b_row = to_row(b_col)

    m_prev = m_ref[...]
    dmat = jnp.where(causal, b_col - b_row + i_row, -jnp.inf)
    inter = b_col + m_prev
    m_row = jnp.maximum(inter, jnp.max(dmat, axis=1, keepdims=True))
    qk = lax.dot_general(q, k, (((1,), (1,)), ((), ())), preferred_element_type=F32) * scale
    w = qk * jnp.exp(dmat - m_row)
    g = jnp.exp(inter - m_row)

    c_old = c_ref[...]
    n_old = n_ref[...]
    q_c = jnp.dot(q, c_old.astype(BF16), preferred_element_type=F32)
    num = jnp.dot(w.astype(BF16), v, preferred_element_type=F32) + g * q_c
    den = (jnp.sum(w, axis=1, keepdims=True)
           + g * jnp.sum(q.astype(F32) * n_old, axis=1, keepdims=True))
    hid = num * (1.0 / jnp.maximum(jnp.abs(den), jnp.exp(-m_row)))
    gate = 1.0 / (1.0 + jnp.exp(-og_ref[...]))
    y_ref[...] = (gate * hid).astype(y_ref.dtype)

    m_new = m_row[ln - 1:ln, :]
    b_last = b_col[ln - 1:ln, :]
    wl_row = jnp.exp(b_last - b_row + i_row - m_new)
    gl = jnp.exp(b_last + m_prev - m_new)
    wl_col = to_col(wl_row) * scale
    v_w = (wl_col * v.astype(F32)).astype(BF16)
    kv = lax.dot_general(k, v_w, (((0,), (0,)), ((), ())), preferred_element_type=F32)
    c_ref[...] = gl * c_old + kv
    n_ref[...] = gl * n_old + jnp.sum(wl_col * k.astype(F32), axis=0, keepdims=True)
    m_ref[...] = m_new


def _mlstm(proj_a, proj_b, gates_i, gates_f, bias, *, batch, seq, d_model, chunk):
    dqk, dv = d_model // 16, d_model // 8
    nc = seq // chunk
    t = batch * seq
    k_blk0 = (d_model // 4) // dqk
    v_blk0 = (d_model // 2) // dv
    gate_spec = pl.BlockSpec((None, None, None, 1, chunk), lambda b, h, c: (b, h, c, 0, 0))
    return pl.pallas_call(
        functools.partial(_mlstm_kernel, scale=float(dqk) ** -0.5),
        grid=(batch, M_HEADS, nc),
        in_specs=[pl.BlockSpec(memory_space=pltpu.SMEM),
                  pl.BlockSpec((chunk, dqk), lambda b, h, c: (b * nc + c, h)),
                  pl.BlockSpec((chunk, dqk), lambda b, h, c: (b * nc + c, k_blk0 + h)),
                  pl.BlockSpec((chunk, dv), lambda b, h, c: (b * nc + c, v_blk0 + h)),
                  pl.BlockSpec((chunk, dv), lambda b, h, c: (b * nc + c, h)),
                  gate_spec, gate_spec],
        out_specs=pl.BlockSpec((chunk, dv), lambda b, h, c: (b * nc + c, h)),
        out_shape=jax.ShapeDtypeStruct((t, M_HEADS * dv), BF16),
        scratch_shapes=[pltpu.VMEM((dqk, dv), F32), pltpu.VMEM((1, dqk), F32), pltpu.VMEM((1, 1), F32)],
        compiler_params=_params("parallel", "parallel", "arbitrary"),
    )(bias, proj_a, proj_a, proj_a, proj_b, gates_i, gates_f)


def _attn_kernel(slope_ref, q_ref, k_ref, v_ref, g_ref, lq1_ref, lk1_ref, lq2_ref, lk2_ref, y_ref,
                 m_ref, l_ref, acc_ref, *, tq, tk, dh, lam_init):
    head = pl.program_id(1)
    qi = pl.program_id(2)
    slope = slope_ref[head]
    scale = float(dh) ** -0.5

    m_ref[...] = jnp.full_like(m_ref, NEG_BIG)
    l_ref[...] = jnp.zeros_like(l_ref)
    acc_ref[...] = jnp.zeros_like(acc_ref)

    q = q_ref[...]
    rel = (lax.broadcasted_iota(jnp.int32, (tq, tk), 0)
           - lax.broadcasted_iota(jnp.int32, (tq, tk), 1)).astype(F32)

    def body(kb, carry):
        ks = pl.multiple_of(kb * tk, tk)
        k_blk = k_ref[pl.ds(ks, tk), :]
        v_blk = v_ref[pl.ds(ks, tk), :]
        dist = rel + (qi * tq - kb * tk).astype(F32)
        bias = jnp.where(dist >= 0.0, -slope * dist, NEG_BIG)
        for mp in range(2):
            s = lax.dot_general(q[:, mp * dh:(mp + 1) * dh], k_blk[:, mp * dh:(mp + 1) * dh],
                                (((1,), (1,)), ((), ())), preferred_element_type=F32) * scale + bias
            m_old = m_ref[mp]
            m_new = jnp.maximum(m_old, jnp.max(s, axis=1, keepdims=True))
            p = jnp.exp(s - m_new)
            alpha = jnp.exp(m_old - m_new)
            l_ref[mp] = alpha * l_ref[mp] + jnp.sum(p, axis=1, keepdims=True)
            acc_ref[mp] = alpha * acc_ref[mp] + jnp.dot(p.astype(BF16), v_blk, preferred_element_type=F32)
            m_ref[mp] = m_new
        return carry

    n_kb = (qi * tq + tq + tk - 1) // tk
    lax.fori_loop(0, n_kb, body, 0)

    lam = (jnp.exp(jnp.sum(lq1_ref[...] * lk1_ref[...], axis=1, keepdims=True))
           - jnp.exp(jnp.sum(lq2_ref[...] * lk2_ref[...], axis=1, keepdims=True)) + lam_init)
    o = acc_ref[0] * (1.0 / l_ref[0]) - lam * (acc_ref[1] * (1.0 / l_ref[1]))
    o = o * lax.rsqrt(jnp.mean(o * o, axis=-1, keepdims=True) + SUBLN_EPS) * g_ref[...]
    y_ref[...] = (o * (1.0 - lam_init)).astype(y_ref.dtype)


def _diff_attn(proj_a, slopes, subln_g, lq1, lk1, lq2, lk2, lam_init, *, batch, seq, d_model, tq=256):
    dh = d_model // 32
    dv = 2 * dh
    tq = _tile(seq, tq)
    nq = seq // tq
    t = batch * seq
    q_blk0 = d_model // dv
    k_blk0 = (5 * d_model // 4) // dv
    v_blk0 = (3 * d_model // 2) // dv
    vec = lambda n: pl.BlockSpec((1, n), lambda b, h, i: (0, 0))
    return pl.pallas_call(
        functools.partial(_attn_kernel, tq=tq, tk=tq, dh=dh, lam_init=lam_init),
        grid=(batch, A_HEADS, nq),
        in_specs=[pl.BlockSpec(memory_space=pltpu.SMEM),
                  pl.BlockSpec((tq, dv), lambda b, h, i: (b * nq + i, q_blk0 + h)),
                  pl.BlockSpec((seq, dv), lambda b, h, i: (b, k_blk0 + h)),
                  pl.BlockSpec((seq, dv), lambda b, h, i: (b, v_blk0 + h)),
                  vec(dv), vec(dh), vec(dh), vec(dh), vec(dh)],
        out_specs=pl.BlockSpec((tq, dv), lambda b, h, i: (b * nq + i, h)),
        out_shape=jax.ShapeDtypeStruct((t, A_HEADS * dv), BF16),
        scratch_shapes=[pltpu.VMEM((2, tq, 1), F32), pltpu.VMEM((2, tq, 1), F32), pltpu.VMEM((2, tq, dv), F32)],
        compiler_params=_params("parallel", "parallel", "arbitrary"),
    )(slopes, proj_a, proj_a, proj_a, subln_g.reshape(1, dv),
      lq1.reshape(1, dh), lk1.reshape(1, dh), lq2.reshape(1, dh), lk2.reshape(1, dh))


def _conv_kernel(cb_ref, cc_ref, cx_ref, w_ref, y_ref):
    z = cc_ref[...] * cx_ref[...]
    rows = lax.broadcasted_iota(jnp.int32, z.shape, 0)
    w = w_ref[...]
    y = w[C_KW - 1:C_KW, :] * z
    for back in range(1, C_KW):
        zs = jnp.where(rows >= back, pltpu.roll(z, back, axis=0), 0.0)
        y = y + w[C_KW - 1 - back:C_KW - back, :] * zs
    y_ref[...] = (cb_ref[...] * y).astype(y_ref.dtype)


def _short_conv(proj_b, conv_w, *, batch, seq, d_model, tc=256):
    cw = d_model // 4
    tc = _tile(cw, tc)
    base = (d_model // 2) // tc
    per = cw // tc
    t = batch * seq
    return pl.pallas_call(
        _conv_kernel,
        grid=(batch, per),
        in_specs=[pl.BlockSpec((seq, tc), lambda b, j: (b, base + j)),
                  pl.BlockSpec((seq, tc), lambda b, j: (b, base + per + j)),
                  pl.BlockSpec((seq, tc), lambda b, j: (b, base + 2 * per + j)),
                  pl.BlockSpec((C_KW, tc), lambda b, j: (0, j))],
        out_specs=pl.BlockSpec((seq, tc), lambda b, j: (b, j)),
        out_shape=jax.ShapeDtypeStruct((t, cw), BF16),
        compiler_params=_params("parallel", "parallel"),
    )(proj_b, proj_b, proj_b, conv_w)


def kernel(x, w_in, m_b_i, m_b_f, a_lq1, a_lk1, a_lq2, a_lk2, a_subln_g, c_conv_w, w_out,
           ln1_g, ln1_b, w_up, w_down, ln2_g, ln2_b):
    batch, seq, d = x.shape
    depth = w_in.shape[0]
    t = batch * seq
    alpha = (2 * depth) ** 0.25
    chunk = min(64, seq)
    nc = seq // chunk

    o_gate = 3 * d // 2
    o_aq = o_gate + N_GATES
    o_cb = o_aq + 3 * d // 4
    w_a = jnp.concatenate([w_in[:, :, :d], w_in[:, :, o_aq:o_cb]], axis=-1).astype(BF16)
    w_b = jnp.concatenate([w_in[:, :, d:o_gate], w_in[:, :, o_cb:]], axis=-1).astype(BF16)
    w_g = jnp.pad(w_in[:, :, o_gate:o_aq], ((0, 0), (0, 0), (0, LANES - N_GATES))).astype(BF16)
    w_out_b = w_out.astype(BF16)
    w_up_b = w_up.astype(BF16)
    w_down_b = w_down.astype(BF16)
    slopes = jnp.asarray([2.0 ** (-8.0 * (i + 1) / A_HEADS) for i in range(A_HEADS)], F32)

    xf = x.reshape(t, d)
    xb = xf.astype(BF16)
    for l in range(depth):
        proj_a = _matmul(xb, w_a[l], BF16)
        proj_b = _matmul(xb, w_b[l], F32)
        gates = _matmul(xb, w_g[l], F32, tn=LANES)[:, :N_GATES]
        gates = gates.reshape(batch, nc, chunk, 2, M_HEADS).transpose(3, 0, 4, 1, 2)
        gates = gates.reshape(2, batch, M_HEADS, nc, 1, chunk)
        bias = jnp.stack([m_b_i[l], m_b_f[l]])

        y_m = _mlstm(proj_a, proj_b, gates[0], gates[1], bias, batch=batch, seq=seq, d_model=d, chunk=chunk)
        lam_init = 0.8 - 0.6 * math.exp(-0.3 * l)
        y_a = _diff_attn(proj_a, slopes, a_subln_g[l], a_lq1[l], a_lk1[l], a_lq2[l], a_lk2[l], lam_init,
                         batch=batch, seq=seq, d_model=d)
        y_c = _short_conv(proj_b, c_conv_w[l], batch=batch, seq=seq, d_model=d)
        y = jnp.concatenate([y_m, y_a, y_c], axis=-1)

        mix = _matmul(y, w_out_b[l], F32)
        xf, xb = _res_ln(xf, mix, ln1_g[l], ln1_b[l], alpha)
        hid = _matmul(xb, w_up_b[l], BF16, relu2=True)
        ff = _matmul_ktiled(hid, w_down_b[l], F32)
        xf, xb = _res_ln(xf, ff, ln2_g[l], ln2_b[l], alpha)
    return xf.reshape(batch, seq, d)
```

```python
import functools
import math

import jax
import jax.numpy as jnp
from jax import lax
from jax.experimental import pallas as pl
from jax.experimental.pallas import tpu as pltpu

M_HEADS = 4
A_HEADS = 4
C_KW = 3
N_GATES = 2 * M_HEADS
LN_EPS = 1e-5
SUBLN_EPS = 1e-5

V7X_VMEM_LIMIT_BYTES = 56 * 1024 * 1024
LANES = 128
NEG_BIG = -1e30
LOG2E = 1.4426950408889634

F32 = jnp.float32
BF16 = jnp.bfloat16


def _tile(n, want):
    t = min(n, want)
    while n % t:
        t //= 2
    return t


def _params(*sem):
    return pltpu.CompilerParams(dimension_semantics=sem, vmem_limit_bytes=V7X_VMEM_LIMIT_BYTES)


def _lane_tile(x, reps):
    return x if reps == 1 else jnp.concatenate([x] * reps, axis=1)


def _mm_kernel(*refs, relu2):
    x_refs, w_ref, o_ref = refs[:-2], refs[-2], refs[-1]
    acc = None
    k0 = 0
    for x_ref in x_refs:
        kp = x_ref.shape[1]
        part = jnp.dot(x_ref[...], w_ref[k0:k0 + kp, :], preferred_element_type=F32)
        acc = part if acc is None else acc + part
        k0 += kp
    if relu2:
        acc = jnp.square(jnp.maximum(acc, 0.0))
    o_ref[...] = acc.astype(o_ref.dtype)


def _matmul(xs, w, layer, out_dtype, *, name, n=None, col0=0, relu2=False, tm=1024, tn=1024):
    m = xs[0].shape[0]
    k = w.shape[1]
    n = w.shape[2] if n is None else n
    tm, tn = _tile(m, tm), _tile(n, tn)
    assert col0 % tn == 0 and sum(xp.shape[1] for xp in xs) == k
    cb0 = col0 // tn
    return pl.pallas_call(
        functools.partial(_mm_kernel, relu2=relu2),
        grid=(m // tm, n // tn),
        in_specs=[pl.BlockSpec((tm, xp.shape[1]), lambda i, j: (i, 0)) for xp in xs]
        + [pl.BlockSpec((None, k, tn), lambda i, j: (layer, 0, cb0 + j))],
        out_specs=pl.BlockSpec((tm, tn), lambda i, j: (i, j)),
        out_shape=jax.ShapeDtypeStruct((m, n), out_dtype),
        compiler_params=_params("parallel", "parallel"),
        name=name,
    )(*xs, w)


def _mm_acc_kernel(x_ref, w_ref, o_ref, acc_ref):
    kk = pl.program_id(2)

    @pl.when(kk == 0)
    def _():
        acc_ref[...] = jnp.zeros_like(acc_ref)

    acc_ref[...] += jnp.dot(x_ref[...], w_ref[...], preferred_element_type=F32)

    @pl.when(kk == pl.num_programs(2) - 1)
    def _():
        o_ref[...] = acc_ref[...].astype(o_ref.dtype)


def _matmul_ktiled(x, w, layer, out_dtype, *, name, tm=1024, tn=1024, tk=2048):
    m, k = x.shape
    n = w.shape[2]
    tm, tn, tk = _tile(m, tm), _tile(n, tn), _tile(k, tk)
    return pl.pallas_call(
        _mm_acc_kernel,
        grid=(m // tm, n // tn, k // tk),
        in_specs=[pl.BlockSpec((tm, tk), lambda i, j, kk: (i, kk)),
                  pl.BlockSpec((None, tk, tn), lambda i, j, kk: (layer, kk, j))],
        out_specs=pl.BlockSpec((tm, tn), lambda i, j, kk: (i, j)),
        out_shape=jax.ShapeDtypeStruct((m, n), out_dtype),
        scratch_shapes=[pltpu.VMEM((tm, tn), F32)],
        compiler_params=_params("parallel", "parallel", "arbitrary"),
        name=name,
    )(x, w)


def _res_ln_kernel(x_ref, y_ref, g_ref, b_ref, o_ref, ob_ref, *, alpha):
    r = alpha * x_ref[...] + y_ref[...]
    mu = jnp.mean(r, axis=-1, keepdims=True)
    d = r - mu
    var = jnp.mean(d * d, axis=-1, keepdims=True)
    o = d * lax.rsqrt(var + LN_EPS) * g_ref[...] + b_ref[...]
    o_ref[...] = o
    ob_ref[...] = o.astype(BF16)


def _res_ln(x, y, g, b, alpha, *, tr=256):
    t, d = x.shape
    tr = _tile(t, tr)
    row = pl.BlockSpec((tr, d), lambda i: (i, 0))
    vec = pl.BlockSpec((1, d), lambda i: (0, 0))
    return pl.pallas_call(
        functools.partial(_res_ln_kernel, alpha=alpha),
        grid=(t // tr,),
        in_specs=[row, row, vec, vec],
        out_specs=[row, row],
        out_shape=[jax.ShapeDtypeStruct((t, d), F32), jax.ShapeDtypeStruct((t, d), BF16)],
        compiler_params=_params("parallel"),
        name="res_ln",
    )(x, y, g.reshape(1, d), b.reshape(1, d))


def _mlstm_kernel(bias_ref, q_ref, k_ref, v_ref, og_ref, gate_ref, y_ref, c_ref, n_ref, m_ref, *, dqk, dv):
    @pl.when(pl.program_id(1) == 0)
    def _():
        c_ref[...] = jnp.zeros_like(c_ref)
        n_ref[...] = jnp.zeros_like(n_ref)
        m_ref[...] = jnp.zeros_like(m_ref)

    ln = q_ref.shape[0]
    scale = float(dqk) ** -0.5
    rows = lax.broadcasted_iota(jnp.int32, (ln, ln), 0)
    cols = lax.broadcasted_iota(jnp.int32, (ln, ln), 1)
    causal = cols <= rows
    eye = cols == rows

    def to_col(row):
        return jnp.sum(jnp.where(eye, jnp.broadcast_to(row, (ln, ln)), 0.0), axis=1, keepdims=True)

    def to_row(col):
        return jnp.sum(jnp.where(eye, jnp.broadcast_to(col, (ln, ln)), 0.0), axis=0, keepdims=True)

    for head in range(M_HEADS):
        q = q_ref[:, head * dqk:(head + 1) * dqk]
        k = k_ref[:, head * dqk:(head + 1) * dqk]
        v = v_ref[:, head * dv:(head + 1) * dv]
        i_row = gate_ref[head:head + 1, :] + bias_ref[0, head]
        f_row = gate_ref[M_HEADS + head:M_HEADS + head + 1, :] + bias_ref[1, head]
        lf_row = jnp.minimum(f_row, 0.0) - jnp.log(1.0 + jnp.exp(-jnp.abs(f_row)))

        b_col = jnp.sum(jnp.where(causal, jnp.broadcast_to(lf_row, (ln, ln)), 0.0), axis=1, keepdims=True)
        b_row = to_row(b_col)

        m_prev = m_ref[head]
        dmat = jnp.where(causal, b_col - b_row + i_row, -jnp.inf)
        inter = b_col + m_prev
        m_row = jnp.maximum(inter, jnp.max(dmat, axis=1, keepdims=True))
        qk = lax.dot_general(q, k, (((1,), (1,)), ((), ())), preferred_element_type=F32) * scale
        w = qk * jnp.exp(dmat - m_row)
        g = jnp.exp(inter - m_row)

        c_old = c_ref[head]
        n_old = n_ref[head]
        q_c = jnp.dot(q, c_old.astype(BF16), preferred_element_type=F32)
        num = jnp.dot(w.astype(BF16), v, preferred_element_type=F32) + g * q_c
        den = (jnp.sum(w, axis=1, keepdims=True)
               + g * jnp.sum(q.astype(F32) * n_old, axis=1, keepdims=True))
        hid = num * (1.0 / jnp.maximum(jnp.abs(den), jnp.exp(-m_row)))
        gate = 1.0 / (1.0 + jnp.exp(-og_ref[:, head * dv:(head + 1) * dv]))
        y_ref[:, head * dv:(head + 1) * dv] = (gate * hid).astype(y_ref.dtype)

        m_new = m_row[ln - 1:ln, :]
        b_last = b_col[ln - 1:ln, :]
        wl_row = jnp.exp(b_last - b_row + i_row - m_new)
        gl = jnp.exp(b_last + m_prev - m_new)
        wl_col = to_col(wl_row) * scale
        v_w = (wl_col * v.astype(F32)).astype(BF16)
        kv = lax.dot_general(k, v_w, (((0,), (0,)), ((), ())), preferred_element_type=F32)
        c_ref[head] = gl * c_old + kv
        n_ref[head] = gl * n_old + jnp.sum(wl_col * k.astype(F32), axis=0, keepdims=True)
        m_ref[head] = m_new


def _mlstm(m_qkv, m_o, gates, bias, *, batch, seq, d_model, chunk):
    dqk, dv = d_model // 16, d_model // 8
    qw, vw = M_HEADS * dqk, M_HEADS * dv
    nc = seq // chunk
    t = batch * seq
    return pl.pallas_call(
        functools.partial(_mlstm_kernel, dqk=dqk, dv=dv),
        grid=(batch, nc),
        in_specs=[pl.BlockSpec(memory_space=pltpu.SMEM),
                  pl.BlockSpec((chunk, qw), lambda b, c: (b * nc + c, 0)),
                  pl.BlockSpec((chunk, qw), lambda b, c: (b * nc + c, 1)),
                  pl.BlockSpec((chunk, vw), lambda b, c: (b * nc + c, 1)),
                  pl.BlockSpec((chunk, vw), lambda b, c: (b * nc + c, 0)),
                  pl.BlockSpec((None, None, N_GATES, chunk), lambda b, c: (b, c, 0, 0))],
        out_specs=pl.BlockSpec((chunk, vw), lambda b, c: (b * nc + c, 0)),
        out_shape=jax.ShapeDtypeStruct((t, vw), BF16),
        scratch_shapes=[pltpu.VMEM((M_HEADS, dqk, dv), F32), pltpu.VMEM((M_HEADS, 1, dqk), F32),
                        pltpu.VMEM((M_HEADS, 1, 1), F32)],
        compiler_params=_params("parallel", "arbitrary"),
        name="mlstm",
    )(bias, m_qkv, m_qkv, m_qkv, m_o, gates)


def _attn_kernel(slope_ref, q_ref, k_ref, v_ref, g_ref, lq1_ref, lk1_ref, lq2_ref, lk2_ref, y_ref,
                 m_ref, l_ref, acc_ref, bias_ref, *, tq, dh, lam_init):
    head = pl.program_id(1)
    qi = pl.program_id(2)
    dv = acc_ref.shape[-1]
    slope2 = slope_ref[head] * LOG2E
    cs = (float(dh) ** -0.5) * LOG2E

    m_ref[...] = jnp.full_like(m_ref, NEG_BIG)
    l_ref[...] = jnp.zeros_like(l_ref)
    acc_ref[...] = jnp.zeros_like(acc_ref)
    rel = (lax.broadcasted_iota(jnp.int32, (tq, tq), 0)
           - lax.broadcasted_iota(jnp.int32, (tq, tq), 1)).astype(F32)
    bias_ref[...] = -slope2 * rel

    q = q_ref[...]

    def step(kb, bias, c0):
        ks = pl.multiple_of(kb * tq, tq)
        k_blk = k_ref[pl.ds(ks, tq), :]
        v_blk = v_ref[pl.ds(ks, tq), :]
        for mp in range(2):
            t2 = lax.dot_general(q[:, mp * dh:(mp + 1) * dh], k_blk[:, mp * dh:(mp + 1) * dh],
                                 (((1,), (1,)), ((), ())), preferred_element_type=F32) * cs + bias
            m_old = m_ref[mp]
            m_new = jnp.maximum(m_old, jnp.max(t2, axis=1, keepdims=True) + c0)
            p = jnp.exp2(t2 - _lane_tile(m_new - c0, tq // LANES))
            alpha = jnp.exp2(m_old - m_new)
            l_ref[mp] = alpha * l_ref[mp] + jnp.sum(p, axis=1, keepdims=True)
            acc_ref[mp] = (_lane_tile(alpha, dv // LANES) * acc_ref[mp]
                           + jnp.dot(p.astype(BF16), v_blk, preferred_element_type=F32))
            m_ref[mp] = m_new

    def body(kb, carry):
        step(kb, bias_ref[...], -slope2 * ((qi - kb) * tq).astype(F32))
        return carry

    lax.fori_loop(0, qi, body, 0)
    step(qi, jnp.where(rel >= 0.0, bias_ref[...], NEG_BIG), 0.0)

    lam = (jnp.exp(jnp.sum(lq1_ref[...] * lk1_ref[...], axis=1, keepdims=True))
           - jnp.exp(jnp.sum(lq2_ref[...] * lk2_ref[...], axis=1, keepdims=True)) + lam_init)
    reps = dv // LANES
    o = (acc_ref[0] * _lane_tile(1.0 / l_ref[0], reps)
         - lam * (acc_ref[1] * _lane_tile(1.0 / l_ref[1], reps)))
    o = o * lax.rsqrt(jnp.mean(o * o, axis=-1, keepdims=True) + SUBLN_EPS) * g_ref[...]
    y_ref[...] = (o * (1.0 - lam_init)).astype(y_ref.dtype)


def _diff_attn(a_qkv, slopes, subln_g, lq1, lk1, lq2, lk2, lam_init, *, batch, seq, d_model, tq=512):
    dh = d_model // 32
    dv = 2 * dh
    tq = _tile(seq, tq)
    nq = seq // tq
    t = batch * seq
    q_blk0 = 0
    k_blk0 = A_HEADS
    v_blk0 = 2 * A_HEADS
    vec = lambda n: pl.BlockSpec((1, n), lambda b, h, i: (0, 0))
    return pl.pallas_call(
        functools.partial(_attn_kernel, tq=tq, dh=dh, lam_init=lam_init),
        grid=(batch, A_HEADS, nq),
        in_specs=[pl.BlockSpec(memory_space=pltpu.SMEM),
                  pl.BlockSpec((tq, dv), lambda b, h, i: (b * nq + i, q_blk0 + h)),
                  pl.BlockSpec((seq, dv), lambda b, h, i: (b, k_blk0 + h)),
                  pl.BlockSpec((seq, dv), lambda b, h, i: (b, v_blk0 + h)),
                  vec(dv), vec(dh), vec(dh), vec(dh), vec(dh)],
        out_specs=pl.BlockSpec((tq, dv), lambda b, h, i: (b * nq + i, h)),
        out_shape=jax.ShapeDtypeStruct((t, A_HEADS * dv), BF16),
        scratch_shapes=[pltpu.VMEM((2, tq, LANES), F32), pltpu.VMEM((2, tq, LANES), F32),
                        pltpu.VMEM((2, tq, dv), F32), pltpu.VMEM((tq, tq), F32)],
        compiler_params=_params("parallel", "parallel", "arbitrary"),
        name="diff_attn",
    )(slopes, a_qkv, a_qkv, a_qkv, subln_g.reshape(1, dv),
      lq1.reshape(1, dh), lk1.reshape(1, dh), lq2.reshape(1, dh), lk2.reshape(1, dh))


def _conv_kernel(cb_ref, cc_ref, cx_ref, w_ref, y_ref):
    z = cc_ref[...] * cx_ref[...]
    rows = lax.broadcasted_iota(jnp.int32, z.shape, 0)
    w = w_ref[...]
    y = w[C_KW - 1:C_KW, :] * z
    for back in range(1, C_KW):
        zs = jnp.where(rows >= back, pltpu.roll(z, back, axis=0), 0.0)
        y = y + w[C_KW - 1 - back:C_KW - back, :] * zs
    y_ref[...] = (cb_ref[...] * y).astype(y_ref.dtype)


def _short_conv(c_bcx, conv_w, *, batch, seq, d_model, tc=256):
    cw = d_model // 4
    tc = _tile(cw, tc)
    per = cw // tc
    t = batch * seq
    return pl.pallas_call(
        _conv_kernel,
        grid=(batch, per),
        in_specs=[pl.BlockSpec((seq, tc), lambda b, j: (b, j)),
                  pl.BlockSpec((seq, tc), lambda b, j: (b, per + j)),
                  pl.BlockSpec((seq, tc), lambda b, j: (b, 2 * per + j)),
                  pl.BlockSpec((C_KW, tc), lambda b, j: (0, j))],
        out_specs=pl.BlockSpec((seq, tc), lambda b, j: (b, j)),
        out_shape=jax.ShapeDtypeStruct((t, cw), BF16),
        compiler_params=_params("parallel", "parallel"),
        name="short_conv",
    )(c_bcx, c_bcx, c_bcx, conv_w)


def kernel(x, w_in, m_b_i, m_b_f, a_lq1, a_lk1, a_lq2, a_lk2, a_subln_g, c_conv_w, w_out,
           ln1_g, ln1_b, w_up, w_down, ln2_g, ln2_b):
    batch, seq, d = x.shape
    depth = w_in.shape[0]
    t = batch * seq
    alpha = (2 * depth) ** 0.25
    chunk = _tile(seq, 256)
    nc = seq // chunk

    o_gate = 3 * d // 2
    o_aq = o_gate + N_GATES
    o_cb = o_aq + 3 * d // 4
    w_in_b = w_in.astype(BF16)
    w_aqkv = w_in_b[:, :, o_aq:o_cb]
    w_conv = w_in_b[:, :, o_cb:]
    w_g = jnp.pad(w_in_b[:, :, o_gate:o_aq], ((0, 0), (0, 0), (0, LANES - N_GATES)))
    w_out_b = w_out.astype(BF16)
    w_up_b = w_up.astype(BF16)
    w_down_b = w_down.astype(BF16)
    slopes = jnp.asarray([2.0 ** (-8.0 * (i + 1) / A_HEADS) for i in range(A_HEADS)], F32)

    xf = x.reshape(t, d)
    xb = xf.astype(BF16)
    for l in range(depth):
        m_qkv = _matmul([xb], w_in_b, l, BF16, name="in_proj_mqkv", n=d)
        m_o = _matmul([xb], w_in_b, l, F32, name="in_proj_mo", n=d // 2, col0=d)
        a_qkv = _matmul([xb], w_aqkv, l, BF16, name="in_proj_aqkv")
        c_bcx = _matmul([xb], w_conv, l, F32, name="in_proj_conv")
        gates = _matmul([xb], w_g, l, F32, name="in_proj_gates", tn=LANES)[:, :N_GATES]
        gates = gates.reshape(batch, nc, chunk, N_GATES).transpose(0, 1, 3, 2)
        bias = jnp.stack([m_b_i[l], m_b_f[l]])

        y_m = _mlstm(m_qkv, m_o, gates, bias, batch=batch, seq=seq, d_model=d, chunk=chunk)
        lam_init = 0.8 - 0.6 * math.exp(-0.3 * l)
        y_a = _diff_attn(a_qkv, slopes, a_subln_g[l], a_lq1[l], a_lk1[l], a_lq2[l], a_lk2[l], lam_init,
                         batch=batch, seq=seq, d_model=d)
        y_c = _short_conv(c_bcx, c_conv_w[l], batch=batch, seq=seq, d_model=d)
        mix = _matmul([y_m, y_a, y_c], w_out_b, l, F32, name="out_proj")
        xf, xb = _res_ln(xf, mix, ln1_g[l], ln1_b[l], alpha)
        hid = _matmul([xb], w_up_b, l, BF16, name="mlp_up", relu2=True)
        ff = _matmul_ktiled(hid, w_down_b, l, F32, name="mlp_down")
        xf, xb = _res_ln(xf, ff, ln2_g[l], ln2_b[l], alpha)
    return xf.reshape(batch, seq, d)
```

```python
import functools
import math

import jax
import jax.numpy as jnp
from jax import lax
from jax.experimental import pallas as pl
from jax.experimental.pallas import tpu as pltpu

M_HEADS = 4
A_HEADS = 4
C_KW = 3
N_GATES = 2 * M_HEADS
LN_EPS = 1e-5
SUBLN_EPS = 1e-5

V7X_VMEM_LIMIT_BYTES = 56 * 1024 * 1024
LANES = 128
NEG_BIG = -1e30
LOG2E = 1.4426950408889634

F32 = jnp.float32
BF16 = jnp.bfloat16


def _tile(n, want):
    t = min(n, want)
    while n % t:
        t //= 2
    return t


def _params(*sem):
    return pltpu.CompilerParams(dimension_semantics=sem, vmem_limit_bytes=V7X_VMEM_LIMIT_BYTES)


def _lane_tile(x, reps):
    return x if reps == 1 else jnp.concatenate([x] * reps, axis=1)


def _mm_kernel(*refs, relu2):
    x_refs, w_ref, o_ref = refs[:-2], refs[-2], refs[-1]
    acc = None
    k0 = 0
    for x_ref in x_refs:
        kp = x_ref.shape[1]
        part = jnp.dot(x_ref[...], w_ref[k0:k0 + kp, :].astype(BF16), preferred_element_type=F32)
        acc = part if acc is None else acc + part
        k0 += kp
    if relu2:
        acc = jnp.square(jnp.maximum(acc, 0.0))
    o_ref[...] = acc.astype(o_ref.dtype)


def _matmul(xs, w, layer, out_dtype, *, name, n=None, col0=0, relu2=False, tm=1024, tn=1024, x_buffers=2):
    m = xs[0].shape[0]
    k = w.shape[1]
    n = w.shape[2] if n is None else n
    tm, tn = _tile(m, tm), _tile(n, tn)
    assert col0 % tn == 0 and sum(xp.shape[1] for xp in xs) == k
    cb0 = col0 // tn
    return pl.pallas_call(
        functools.partial(_mm_kernel, relu2=relu2),
        grid=(m // tm, n // tn),
        in_specs=[pl.BlockSpec((tm, xp.shape[1]), lambda i, j: (i, 0), pipeline_mode=pl.Buffered(x_buffers))
                  for xp in xs]
        + [pl.BlockSpec((None, k, tn), lambda i, j: (layer, 0, cb0 + j))],
        out_specs=pl.BlockSpec((tm, tn), lambda i, j: (i, j)),
        out_shape=jax.ShapeDtypeStruct((m, n), out_dtype),
        compiler_params=_params("parallel", "parallel"),
        name=name,
    )(*xs, w)


def _mm_acc_kernel(x_ref, w_ref, o_ref):
    @pl.when(pl.program_id(2) == 0)
    def _():
        o_ref[...] = jnp.zeros_like(o_ref)

    o_ref[...] += jnp.dot(x_ref[...], w_ref[...].astype(BF16), preferred_element_type=F32)


def _matmul_ktiled(x, w, layer, *, name, tm=2048, tn=1024, tk=1024):
    m, k = x.shape
    n = w.shape[2]
    tm, tn, tk = _tile(m, tm), _tile(n, tn), _tile(k, tk)
    return pl.pallas_call(
        _mm_acc_kernel,
        grid=(m // tm, n // tn, k // tk),
        in_specs=[pl.BlockSpec((tm, tk), lambda i, j, kk: (i, kk)),
                  pl.BlockSpec((None, tk, tn), lambda i, j, kk: (layer, kk, j))],
        out_specs=pl.BlockSpec((tm, tn), lambda i, j, kk: (i, j)),
        out_shape=jax.ShapeDtypeStruct((m, n), F32),
        compiler_params=_params("parallel", "parallel", "arbitrary"),
        name=name,
    )(x, w)


def _res_ln_kernel(x_ref, y_ref, g_ref, b_ref, o_ref, ob_ref, *, alpha):
    r = alpha * x_ref[...] + y_ref[...]
    mu = jnp.mean(r, axis=-1, keepdims=True)
    d = r - mu
    var = jnp.mean(d * d, axis=-1, keepdims=True)
    o = d * lax.rsqrt(var + LN_EPS) * g_ref[...] + b_ref[...]
    o_ref[...] = o
    ob_ref[...] = o.astype(BF16)


def _res_ln(x, y, g, b, alpha, *, tr=256):
    t, d = x.shape
    tr = _tile(t, tr)
    row = pl.BlockSpec((tr, d), lambda i: (i, 0))
    vec = pl.BlockSpec((1, d), lambda i: (0, 0))
    return pl.pallas_call(
        functools.partial(_res_ln_kernel, alpha=alpha),
        grid=(t // tr,),
        in_specs=[row, row, vec, vec],
        out_specs=[row, row],
        out_shape=[jax.ShapeDtypeStruct((t, d), F32), jax.ShapeDtypeStruct((t, d), BF16)],
        compiler_params=_params("parallel"),
        name="res_ln",
    )(x, y, g.reshape(1, d), b.reshape(1, d))


def _mlstm_kernel(bias_ref, q_ref, k_ref, v_ref, og_ref, gate_ref, y_ref, c_ref, n_ref, m_ref, *, dqk, dv):
    @pl.when(pl.program_id(1) == 0)
    def _():
        c_ref[...] = jnp.zeros_like(c_ref)
        n_ref[...] = jnp.zeros_like(n_ref)
        m_ref[...] = jnp.zeros_like(m_ref)

    ln = q_ref.shape[0]
    scale = float(dqk) ** -0.5
    rows = lax.broadcasted_iota(jnp.int32, (ln, ln), 0)
    cols = lax.broadcasted_iota(jnp.int32, (ln, ln), 1)
    causal = cols <= rows
    eye = cols == rows

    def to_col(row):
        return jnp.sum(jnp.where(eye, jnp.broadcast_to(row, (ln, ln)), 0.0), axis=1, keepdims=True)

    def to_row(col):
        return jnp.sum(jnp.where(eye, jnp.broadcast_to(col, (ln, ln)), 0.0), axis=0, keepdims=True)

    for head in range(M_HEADS):
        q = q_ref[:, head * dqk:(head + 1) * dqk]
        k = k_ref[:, head * dqk:(head + 1) * dqk]
        v = v_ref[:, head * dv:(head + 1) * dv]
        i_row = gate_ref[head:head + 1, :] + bias_ref[0, head]
        f_row = gate_ref[M_HEADS + head:M_HEADS + head + 1, :] + bias_ref[1, head]
        lf_row = jnp.minimum(f_row, 0.0) - jnp.log(1.0 + jnp.exp(-jnp.abs(f_row)))

        b_col = jnp.sum(jnp.where(causal, jnp.broadcast_to(lf_row, (ln, ln)), 0.0), axis=1, keepdims=True)
        b_row = to_row(b_col)

        m_prev = m_ref[head]
        dmat = jnp.where(causal, b_col - b_row + i_row, -jnp.inf)
        inter = b_col + m_prev
        m_row = jnp.maximum(inter, jnp.max(dmat, axis=1, keepdims=True))
        qk = lax.dot_general(q, k, (((1,), (1,)), ((), ())), preferred_element_type=F32) * scale
        w = qk * jnp.exp(dmat - m_row)
        g = jnp.exp(inter - m_row)

        c_old = c_ref[head]
        n_old = n_ref[head]
        q_c = jnp.dot(q, c_old.astype(BF16), preferred_element_type=F32)
        num = jnp.dot(w.astype(BF16), v, preferred_element_type=F32) + g * q_c
        den = (jnp.sum(w, axis=1, keepdims=True)
               + g * jnp.sum(q.astype(F32) * n_old, axis=1, keepdims=True))
        hid = num * (1.0 / jnp.maximum(jnp.abs(den), jnp.exp(-m_row)))
        gate = 1.0 / (1.0 + jnp.exp(-og_ref[:, head * dv:(head + 1) * dv]))
        y_ref[:, head * dv:(head + 1) * dv] = (gate * hid).astype(y_ref.dtype)

        m_new = m_row[ln - 1:ln, :]
        b_last = b_col[ln - 1:ln, :]
        wl_row = jnp.exp(b_last - b_row + i_row - m_new)
        gl = jnp.exp(b_last + m_prev - m_new)
        wl_col = to_col(wl_row) * scale
        v_w = (wl_col * v.astype(F32)).astype(BF16)
        kv = lax.dot_general(k, v_w, (((0,), (0,)), ((), ())), preferred_element_type=F32)
        c_ref[head] = gl * c_old + kv
        n_ref[head] = gl * n_old + jnp.sum(wl_col * k.astype(F32), axis=0, keepdims=True)
        m_ref[head] = m_new


def _mlstm(m_qkv, m_o, gates, bias, *, batch, seq, d_model, chunk):
    dqk, dv = d_model // 16, d_model // 8
    qw, vw = M_HEADS * dqk, M_HEADS * dv
    nc = seq // chunk
    t = batch * seq
    return pl.pallas_call(
        functools.partial(_mlstm_kernel, dqk=dqk, dv=dv),
        grid=(batch, nc),
        in_specs=[pl.BlockSpec(memory_space=pltpu.SMEM),
                  pl.BlockSpec((chunk, qw), lambda b, c: (b * nc + c, 0)),
                  pl.BlockSpec((chunk, qw), lambda b, c: (b * nc + c, 1)),
                  pl.BlockSpec((chunk, vw), lambda b, c: (b * nc + c, 1)),
                  pl.BlockSpec((chunk, vw), lambda b, c: (b * nc + c, 0)),
                  pl.BlockSpec((None, None, N_GATES, chunk), lambda b, c: (b, c, 0, 0))],
        out_specs=pl.BlockSpec((chunk, vw), lambda b, c: (b * nc + c, 0)),
        out_shape=jax.ShapeDtypeStruct((t, vw), BF16),
        scratch_shapes=[pltpu.VMEM((M_HEADS, dqk, dv), F32), pltpu.VMEM((M_HEADS, 1, dqk), F32),
                        pltpu.VMEM((M_HEADS, 1, 1), F32)],
        compiler_params=_params("parallel", "arbitrary"),
        name="mlstm",
    )(bias, m_qkv, m_qkv, m_qkv, m_o, gates)


def _attn_kernel(slope_ref, q_ref, k_ref, v_ref, g_ref, lq1_ref, lk1_ref, lq2_ref, lk2_ref, y_ref,
                 m_ref, l_ref, acc_ref, bias_ref, *, th, dh, lam_init):
    head = pl.program_id(1)
    qi = pl.program_id(2)
    dv = acc_ref.shape[-1]
    slope2 = slope_ref[head] * LOG2E
    cs = (float(dh) ** -0.5) * LOG2E

    m_ref[...] = jnp.full_like(m_ref, NEG_BIG)
    l_ref[...] = jnp.zeros_like(l_ref)
    acc_ref[...] = jnp.zeros_like(acc_ref)
    rel = (lax.broadcasted_iota(jnp.int32, (th, th), 0)
           - lax.broadcasted_iota(jnp.int32, (th, th), 1)).astype(F32)
    bias_ref[...] = -slope2 * rel

    def half_step(hf, k_blk, v_blk, bias, c0):
        q = q_ref[hf * th:(hf + 1) * th, :]
        for mp in range(2):
            t2 = lax.dot_general(q[:, mp * dh:(mp + 1) * dh], k_blk[:, mp * dh:(mp + 1) * dh],
                                 (((1,), (1,)), ((), ())), preferred_element_type=F32) * cs + bias
            m_old = m_ref[hf, mp]
            m_new = jnp.maximum(m_old, jnp.max(t2, axis=1, keepdims=True) + c0)
            p = jnp.exp2(t2 - _lane_tile(m_new - c0, th // LANES))
            alpha = jnp.exp2(m_old - m_new)
            p_lanes = p[:, :LANES]
            for c in range(1, th // LANES):
                p_lanes = p_lanes + p[:, c * LANES:(c + 1) * LANES]
            l_ref[hf, mp] = alpha * l_ref[hf, mp] + p_lanes
            acc_ref[hf, mp] = (_lane_tile(alpha, dv // LANES) * acc_ref[hf, mp]
                               + jnp.dot(p.astype(BF16), v_blk, preferred_element_type=F32))
            m_ref[hf, mp] = m_new

    def kv_block(kb):
        ks = pl.multiple_of(kb * th, th)
        return k_ref[pl.ds(ks, th), :], v_ref[pl.ds(ks, th), :]

    def offset(n_blocks):
        return -slope2 * (n_blocks * th).astype(F32)

    def body(kb, carry):
        k_blk, v_blk = kv_block(kb)
        bias = bias_ref[...]
        half_step(0, k_blk, v_blk, bias, offset(2 * qi - kb))
        half_step(1, k_blk, v_blk, bias, offset(2 * qi + 1 - kb))
        return carry

    lax.fori_loop(0, 2 * qi, body, 0)
    bias = bias_ref[...]
    diag_bias = jnp.where(rel >= 0.0, bias, NEG_BIG)
    k_blk, v_blk = kv_block(2 * qi)
    half_step(0, k_blk, v_blk, diag_bias, 0.0)
    half_step(1, k_blk, v_blk, bias, -slope2 * float(th))
    k_blk, v_blk = kv_block(2 * qi + 1)
    half_step(1, k_blk, v_blk, diag_bias, 0.0)

    lam = (jnp.exp(jnp.sum(lq1_ref[...] * lk1_ref[...], axis=1, keepdims=True))
           - jnp.exp(jnp.sum(lq2_ref[...] * lk2_ref[...], axis=1, keepdims=True)) + lam_init)
    for hf in range(2):
        inv = [1.0 / jnp.sum(l_ref[hf, mp], axis=1, keepdims=True) for mp in range(2)]
        o = acc_ref[hf, 0] * inv[0] - lam * (acc_ref[hf, 1] * inv[1])
        o = o * lax.rsqrt(jnp.mean(o * o, axis=-1, keepdims=True) + SUBLN_EPS) * g_ref[...]
        y_ref[hf * th:(hf + 1) * th, :] = (o * (1.0 - lam_init)).astype(y_ref.dtype)


def _diff_attn(a_qkv, slopes, subln_g, lq1, lk1, lq2, lk2, lam_init, *, batch, seq, d_model, th=512):
    dh = d_model // 32
    dv = 2 * dh
    th = _tile(seq // 2, th)
    tq = 2 * th
    nq = seq // tq
    t = batch * seq
    k_blk0 = A_HEADS
    v_blk0 = 2 * A_HEADS
    vec = lambda n: pl.BlockSpec((1, n), lambda b, h, i: (0, 0))
    return pl.pallas_call(
        functools.partial(_attn_kernel, th=th, dh=dh, lam_init=lam_init),
        grid=(batch, A_HEADS, nq),
        in_specs=[pl.BlockSpec(memory_space=pltpu.SMEM),
                  pl.BlockSpec((tq, dv), lambda b, h, i: (b * nq + i, h)),
                  pl.BlockSpec((seq, dv), lambda b, h, i: (b, k_blk0 + h)),
                  pl.BlockSpec((seq, dv), lambda b, h, i: (b, v_blk0 + h)),
                  vec(dv), vec(dh), vec(dh), vec(dh), vec(dh)],
        out_specs=pl.BlockSpec((tq, dv), lambda b, h, i: (b * nq + i, h)),
        out_shape=jax.ShapeDtypeStruct((t, A_HEADS * dv), BF16),
        scratch_shapes=[pltpu.VMEM((2, 2, th, LANES), F32), pltpu.VMEM((2, 2, th, LANES), F32),
                        pltpu.VMEM((2, 2, th, dv), F32), pltpu.VMEM((th, th), F32)],
        compiler_params=_params("parallel", "parallel", "arbitrary"),
        name="diff_attn",
    )(slopes, a_qkv, a_qkv, a_qkv, subln_g.reshape(1, dv),
      lq1.reshape(1, dh), lk1.reshape(1, dh), lq2.reshape(1, dh), lk2.reshape(1, dh))


def _conv_kernel(cb_ref, cc_ref, cx_ref, w_ref, y_ref):
    z = cc_ref[...] * cx_ref[...]
    rows = lax.broadcasted_iota(jnp.int32, z.shape, 0)
    w = w_ref[...]
    y = w[C_KW - 1:C_KW, :] * z
    for back in range(1, C_KW):
        zs = jnp.where(rows >= back, pltpu.roll(z, back, axis=0), 0.0)
        y = y + w[C_KW - 1 - back:C_KW - back, :] * zs
    y_ref[...] = (cb_ref[...] * y).astype(y_ref.dtype)


def _short_conv(c_bcx, conv_w, *, batch, seq, d_model, tc=256):
    cw = d_model // 4
    tc = _tile(cw, tc)
    per = cw // tc
    t = batch * seq
    return pl.pallas_call(
        _conv_kernel,
        grid=(batch, per),
        in_specs=[pl.BlockSpec((seq, tc), lambda b, j: (b, j)),
                  pl.BlockSpec((seq, tc), lambda b, j: (b, per + j)),
                  pl.BlockSpec((seq, tc), lambda b, j: (b, 2 * per + j)),
                  pl.BlockSpec((C_KW, tc), lambda b, j: (0, j))],
        out_specs=pl.BlockSpec((seq, tc), lambda b, j: (b, j)),
        out_shape=jax.ShapeDtypeStruct((t, cw), BF16),
        compiler_params=_params("parallel", "parallel"),
        name="short_conv",
    )(c_bcx, c_bcx, c_bcx, conv_w)


def kernel(x, w_in, m_b_i, m_b_f, a_lq1, a_lk1, a_lq2, a_lk2, a_subln_g, c_conv_w, w_out,
           ln1_g, ln1_b, w_up, w_down, ln2_g, ln2_b):
    batch, seq, d = x.shape
    depth = w_in.shape[0]
    t = batch * seq
    alpha = (2 * depth) ** 0.25
    chunk = _tile(seq, 256)
    nc = seq // chunk

    o_gate = 3 * d // 2
    o_aq = o_gate + N_GATES
    o_cb = o_aq + 3 * d // 4
    w_in_b = w_in.astype(BF16)
    w_tail = w_in_b[:, :, o_aq:]
    w_g = jnp.pad(w_in_b[:, :, o_gate:o_aq], ((0, 0), (0, 0), (0, LANES - N_GATES)))
    w_out_b = w_out.astype(BF16)
    slopes = jnp.asarray([2.0 ** (-8.0 * (i + 1) / A_HEADS) for i in range(A_HEADS)], F32)

    xf = x.reshape(t, d)
    xb = xf.astype(BF16)
    for l in range(depth):
        m_qkv = _matmul([xb], w_in_b, l, BF16, name="in_proj_mqkv", n=d)
        m_o = _matmul([xb], w_in_b, l, F32, name="in_proj_mo", n=d // 2, col0=d)
        a_qkv = _matmul([xb], w_tail, l, BF16, name="in_proj_aqkv", n=3 * d // 4)
        c_bcx = _matmul([xb], w_tail, l, F32, name="in_proj_conv", n=3 * d // 4, col0=3 * d // 4)
        gates = _matmul([xb], w_g, l, F32, name="in_proj_gates", tn=LANES)[:, :N_GATES]
        gates = gates.reshape(batch, nc, chunk, N_GATES).transpose(0, 1, 3, 2)
        bias = jnp.stack([m_b_i[l], m_b_f[l]])

        y_m = _mlstm(m_qkv, m_o, gates, bias, batch=batch, seq=seq, d_model=d, chunk=chunk)
        lam_init = 0.8 - 0.6 * math.exp(-0.3 * l)
        y_a = _diff_attn(a_qkv, slopes, a_subln_g[l], a_lq1[l], a_lk1[l], a_lq2[l], a_lk2[l], lam_init,
                         batch=batch, seq=seq, d_model=d)
        y_c = _short_conv(c_bcx, c_conv_w[l], batch=batch, seq=seq, d_model=d)
        mix = _matmul([y_m, y_a, y_c], w_out_b, l, F32, name="out_proj")
        xf, xb = _res_ln(xf, mix, ln1_g[l], ln1_b[l], alpha)
        hid = _matmul([xb], w_up, l, BF16, name="mlp_up", relu2=True, tm=2048, tn=512, x_buffers=1)
        ff = _matmul_ktiled(hid, w_down, l, name="mlp_down")
        xf, xb = _res_ln(xf, ff, ln2_g[l], ln2_b[l], alpha)
    return xf.reshape(batch, seq, d)
```

```python
import functools
import math

import jax
import jax.numpy as jnp
from jax import lax
from jax.experimental import pallas as pl
from jax.experimental.pallas import tpu as pltpu

M_HEADS = 4
A_HEADS = 4
C_KW = 3
N_GATES = 2 * M_HEADS
LN_EPS = 1e-5
SUBLN_EPS = 1e-5

V7X_VMEM_LIMIT_BYTES = 56 * 1024 * 1024
LANES = 128
NEG_BIG = -1e30
LOG2E = 1.4426950408889634

F32 = jnp.float32
BF16 = jnp.bfloat16


def _tile(n, want):
    t = min(n, want)
    while n % t:
        t //= 2
    return t


def _params(*sem):
    return pltpu.CompilerParams(dimension_semantics=sem, vmem_limit_bytes=V7X_VMEM_LIMIT_BYTES)


def _lane_tile(x, reps):
    return x if reps == 1 else jnp.concatenate([x] * reps, axis=1)


def _mm_kernel(*refs, relu2):
    x_refs, w_ref, o_ref = refs[:-2], refs[-2], refs[-1]
    acc = None
    k0 = 0
    for x_ref in x_refs:
        kp = x_ref.shape[1]
        part = jnp.dot(x_ref[...], w_ref[k0:k0 + kp, :], preferred_element_type=F32)
        acc = part if acc is None else acc + part
        k0 += kp
    if relu2:
        acc = jnp.square(jnp.maximum(acc, 0.0))
    o_ref[...] = acc.astype(o_ref.dtype)


def _matmul(xs, w, out_dtype, *, name, n=None, col0=0, relu2=False, tm=1024, tn=1024):
    m = xs[0].shape[0]
    k = w.shape[0]
    n = w.shape[1] if n is None else n
    tm, tn = _tile(m, tm), _tile(n, tn)
    assert col0 % tn == 0 and sum(xp.shape[1] for xp in xs) == k
    cb0 = col0 // tn
    return pl.pallas_call(
        functools.partial(_mm_kernel, relu2=relu2),
        grid=(m // tm, n // tn),
        in_specs=[pl.BlockSpec((tm, xp.shape[1]), lambda i, j: (i, 0)) for xp in xs]
        + [pl.BlockSpec((k, tn), lambda i, j: (0, cb0 + j))],
        out_specs=pl.BlockSpec((tm, tn), lambda i, j: (i, j)),
        out_shape=jax.ShapeDtypeStruct((m, n), out_dtype),
        compiler_params=_params("parallel", "parallel"),
        name=name,
    )(*xs, w)


def _mm_acc_kernel(x_ref, w_ref, o_ref):
    @pl.when(pl.program_id(2) == 0)
    def _():
        o_ref[...] = jnp.zeros_like(o_ref)

    o_ref[...] += jnp.dot(x_ref[...], w_ref[...], preferred_element_type=F32)


def _matmul_ktiled(x, w, *, name, tm=2048, tn=1024, tk=1024):
    m, k = x.shape
    n = w.shape[1]
    tm, tn, tk = _tile(m, tm), _tile(n, tn), _tile(k, tk)
    return pl.pallas_call(
        _mm_acc_kernel,
        grid=(m // tm, n // tn, k // tk),
        in_specs=[pl.BlockSpec((tm, tk), lambda i, j, kk: (i, kk)),
                  pl.BlockSpec((tk, tn), lambda i, j, kk: (kk, j))],
        out_specs=pl.BlockSpec((tm, tn), lambda i, j, kk: (i, j)),
        out_shape=jax.ShapeDtypeStruct((m, n), F32),
        compiler_params=_params("parallel", "parallel", "arbitrary"),
        name=name,
    )(x, w)


def _res_ln_kernel(x_ref, y_ref, g_ref, b_ref, o_ref, ob_ref, *, alpha):
    r = alpha * x_ref[...] + y_ref[...]
    mu = jnp.mean(r, axis=-1, keepdims=True)
    d = r - mu
    var = jnp.mean(d * d, axis=-1, keepdims=True)
    o = d * lax.rsqrt(var + LN_EPS) * g_ref[...] + b_ref[...]
    o_ref[...] = o
    ob_ref[...] = o.astype(BF16)


def _res_ln(x, y, g, b, alpha, *, tr=256):
    t, d = x.shape
    tr = _tile(t, tr)
    row = pl.BlockSpec((tr, d), lambda i: (i, 0))
    vec = pl.BlockSpec((1, d), lambda i: (0, 0))
    return pl.pallas_call(
        functools.partial(_res_ln_kernel, alpha=alpha),
        grid=(t // tr,),
        in_specs=[row, row, vec, vec],
        out_specs=[row, row],
        out_shape=[jax.ShapeDtypeStruct((t, d), F32), jax.ShapeDtypeStruct((t, d), BF16)],
        compiler_params=_params("parallel"),
        name="res_ln",
    )(x, y, g.reshape(1, d), b.reshape(1, d))


def _cast_side_stream(casts, n_steps, step_of):
    operands, in_specs, out_specs, out_shapes = [], [], [], []
    for w, layer in casts:
        _, r, c = w.shape
        rows = r // n_steps
        assert rows * n_steps == r and rows % 16 == 0
        operands.append(w)
        in_specs.append(pl.BlockSpec((None, rows, c), lambda *g, layer=layer: (layer, step_of(*g), 0)))
        out_specs.append(pl.BlockSpec((rows, c), lambda *g: (step_of(*g), 0)))
        out_shapes.append(jax.ShapeDtypeStruct((r, c), BF16))
    return operands, in_specs, out_specs, out_shapes


def _round_weights(src_refs, dst_refs):
    for src, dst in zip(src_refs, dst_refs, strict=True):
        dst[...] = src[...].astype(dst.dtype)


def _mlstm_kernel(*refs, dqk, dv, n_cast):
    bias_ref, q_ref, k_ref, v_ref, og_ref, gate_ref = refs[:6]
    y_ref = refs[6 + n_cast]
    c_ref, n_ref, m_ref = refs[7 + 2 * n_cast:]
    _round_weights(refs[6:6 + n_cast], refs[7 + n_cast:7 + 2 * n_cast])

    @pl.when(pl.program_id(1) == 0)
    def _():
        c_ref[...] = jnp.zeros_like(c_ref)
        n_ref[...] = jnp.zeros_like(n_ref)
        m_ref[...] = jnp.zeros_like(m_ref)

    ln = q_ref.shape[0]
    scale = float(dqk) ** -0.5
    rows = lax.broadcasted_iota(jnp.int32, (ln, ln), 0)
    cols = lax.broadcasted_iota(jnp.int32, (ln, ln), 1)
    causal = cols <= rows
    eye = cols == rows

    def to_col(row):
        return jnp.sum(jnp.where(eye, jnp.broadcast_to(row, (ln, ln)), 0.0), axis=1, keepdims=True)

    def to_row(col):
        return jnp.sum(jnp.where(eye, jnp.broadcast_to(col, (ln, ln)), 0.0), axis=0, keepdims=True)

    for head in range(M_HEADS):
        q = q_ref[:, head * dqk:(head + 1) * dqk]
        k = k_ref[:, head * dqk:(head + 1) * dqk]
        v = v_ref[:, head * dv:(head + 1) * dv]
        i_row = gate_ref[head:head + 1, :] + bias_ref[0, head]
        f_row = gate_ref[M_HEADS + head:M_HEADS + head + 1, :] + bias_ref[1, head]
        lf_row = jnp.minimum(f_row, 0.0) - jnp.log(1.0 + jnp.exp(-jnp.abs(f_row)))

        b_col = jnp.sum(jnp.where(causal, jnp.broadcast_to(lf_row, (ln, ln)), 0.0), axis=1, keepdims=True)
        b_row = to_row(b_col)

        m_prev = m_ref[head]
        dmat = jnp.where(causal, b_col - b_row + i_row, -jnp.inf)
        inter = b_col + m_prev
        m_row = jnp.maximum(inter, jnp.max(dmat, axis=1, keepdims=True))
        qk = lax.dot_general(q, k, (((1,), (1,)), ((), ())), preferred_element_type=F32) * scale
        w = qk * jnp.exp(dmat - m_row)
        g = jnp.exp(inter - m_row)

        c_old = c_ref[head]
        n_old = n_ref[head]
        q_c = jnp.dot(q, c_old.astype(BF16), preferred_element_type=F32)
        num = jnp.dot(w.astype(BF16), v, preferred_element_type=F32) + g * q_c
        den = (jnp.sum(w, axis=1, keepdims=True)
               + g * jnp.sum(q.astype(F32) * n_old, axis=1, keepdims=True))
        hid = num * (1.0 / jnp.maximum(jnp.abs(den), jnp.exp(-m_row)))
        gate = 1.0 / (1.0 + jnp.exp(-og_ref[:, head * dv:(head + 1) * dv]))
        y_ref[:, head * dv:(head + 1) * dv] = (gate * hid).astype(y_ref.dtype)

        m_new = m_row[ln - 1:ln, :]
        b_last = b_col[ln - 1:ln, :]
        wl_row = jnp.exp(b_last - b_row + i_row - m_new)
        gl = jnp.exp(b_last + m_prev - m_new)
        wl_col = to_col(wl_row) * scale
        v_w = (wl_col * v.astype(F32)).astype(BF16)
        kv = lax.dot_general(k, v_w, (((0,), (0,)), ((), ())), preferred_element_type=F32)
        c_ref[head] = gl * c_old + kv
        n_ref[head] = gl * n_old + jnp.sum(wl_col * k.astype(F32), axis=0, keepdims=True)
        m_ref[head] = m_new


def _mlstm(m_qkv, m_o, gates, bias, casts, *, batch, seq, d_model, chunk):
    dqk, dv = d_model // 16, d_model // 8
    qw, vw = M_HEADS * dqk, M_HEADS * dv
    nc = seq // chunk
    t = batch * seq
    c_ops, c_in, c_out, c_shapes = _cast_side_stream(casts, batch * nc, lambda b, c: b * nc + c)
    return pl.pallas_call(
        functools.partial(_mlstm_kernel, dqk=dqk, dv=dv, n_cast=len(casts)),
        grid=(batch, nc),
        in_specs=[pl.BlockSpec(memory_space=pltpu.SMEM),
                  pl.BlockSpec((chunk, qw), lambda b, c: (b * nc + c, 0)),
                  pl.BlockSpec((chunk, qw), lambda b, c: (b * nc + c, 1)),
                  pl.BlockSpec((chunk, vw), lambda b, c: (b * nc + c, 1)),
                  pl.BlockSpec((chunk, vw), lambda b, c: (b * nc + c, 0)),
                  pl.BlockSpec((None, None, N_GATES, chunk), lambda b, c: (b, c, 0, 0))] + c_in,
        out_specs=[pl.BlockSpec((chunk, vw), lambda b, c: (b * nc + c, 0))] + c_out,
        out_shape=[jax.ShapeDtypeStruct((t, vw), BF16)] + c_shapes,
        scratch_shapes=[pltpu.VMEM((M_HEADS, dqk, dv), F32), pltpu.VMEM((M_HEADS, 1, dqk), F32),
                        pltpu.VMEM((M_HEADS, 1, 1), F32)],
        compiler_params=_params("arbitrary", "arbitrary"),
        name="mlstm",
    )(bias, m_qkv, m_qkv, m_qkv, m_o, gates, *c_ops)


def _attn_kernel(*refs, th, dh, lam_init, n_cast):
    slope_ref, q_ref, k_ref, v_ref, g_ref, lq1_ref, lk1_ref, lq2_ref, lk2_ref = refs[:9]
    y_ref = refs[9 + n_cast]
    m_ref, l_ref, acc_ref, bias_ref = refs[10 + 2 * n_cast:]
    _round_weights(refs[9:9 + n_cast], refs[10 + n_cast:10 + 2 * n_cast])

    head = pl.program_id(1)
    qi = pl.program_id(2)
    dv = acc_ref.shape[-1]
    slope2 = slope_ref[head] * LOG2E
    cs = (float(dh) ** -0.5) * LOG2E

    m_ref[...] = jnp.full_like(m_ref, NEG_BIG)
    l_ref[...] = jnp.zeros_like(l_ref)
    acc_ref[...] = jnp.zeros_like(acc_ref)
    rel = (lax.broadcasted_iota(jnp.int32, (th, th), 0)
           - lax.broadcasted_iota(jnp.int32, (th, th), 1)).astype(F32)
    bias_ref[...] = -slope2 * rel

    def half_step(hf, k_blk, v_blk, bias, c0):
        q = q_ref[hf * th:(hf + 1) * th, :]
        for mp in range(2):
            t2 = lax.dot_general(q[:, mp * dh:(mp + 1) * dh], k_blk[:, mp * dh:(mp + 1) * dh],
                                 (((1,), (1,)), ((), ())), preferred_element_type=F32) * cs + bias
            m_old = m_ref[hf, mp]
            m_new = jnp.maximum(m_old, jnp.max(t2, axis=1, keepdims=True) + c0)
            p = jnp.exp2(t2 - _lane_tile(m_new - c0, th // LANES))
            alpha = jnp.exp2(m_old - m_new)
            p_lanes = p[:, :LANES]
            for c in range(1, th // LANES):
                p_lanes = p_lanes + p[:, c * LANES:(c + 1) * LANES]
            l_ref[hf, mp] = alpha * l_ref[hf, mp] + p_lanes
            acc_ref[hf, mp] = (_lane_tile(alpha, dv // LANES) * acc_ref[hf, mp]
                               + jnp.dot(p.astype(BF16), v_blk, preferred_element_type=F32))
            m_ref[hf, mp] = m_new

    def kv_block(kb):
        ks = pl.multiple_of(kb * th, th)
        return k_ref[pl.ds(ks, th), :], v_ref[pl.ds(ks, th), :]

    def offset(n_blocks):
        return -slope2 * (n_blocks * th).astype(F32)

    def body(kb, carry):
        k_blk, v_blk = kv_block(kb)
        bias = bias_ref[...]
        half_step(0, k_blk, v_blk, bias, offset(2 * qi - kb))
        half_step(1, k_blk, v_blk, bias, offset(2 * qi + 1 - kb))
        return carry

    lax.fori_loop(0, 2 * qi, body, 0)
    bias = bias_ref[...]
    diag_bias = jnp.where(rel >= 0.0, bias, NEG_BIG)
    k_blk, v_blk = kv_block(2 * qi)
    half_step(0, k_blk, v_blk, diag_bias, 0.0)
    half_step(1, k_blk, v_blk, bias, -slope2 * float(th))
    k_blk, v_blk = kv_block(2 * qi + 1)
    half_step(1, k_blk, v_blk, diag_bias, 0.0)

    lam = (jnp.exp(jnp.sum(lq1_ref[...] * lk1_ref[...], axis=1, keepdims=True))
           - jnp.exp(jnp.sum(lq2_ref[...] * lk2_ref[...], axis=1, keepdims=True)) + lam_init)
    for hf in range(2):
        inv = [1.0 / jnp.sum(l_ref[hf, mp], axis=1, keepdims=True) for mp in range(2)]
        o = acc_ref[hf, 0] * inv[0] - lam * (acc_ref[hf, 1] * inv[1])
        o = o * lax.rsqrt(jnp.mean(o * o, axis=-1, keepdims=True) + SUBLN_EPS) * g_ref[...]
        y_ref[hf * th:(hf + 1) * th, :] = (o * (1.0 - lam_init)).astype(y_ref.dtype)


def _diff_attn(a_qkv, slopes, subln_g, lq1, lk1, lq2, lk2, lam_init, casts, *, batch, seq, d_model, th=512):
    dh = d_model // 32
    dv = 2 * dh
    th = _tile(seq // 2, th)
    tq = 2 * th
    nq = seq // tq
    t = batch * seq
    k_blk0 = A_HEADS
    v_blk0 = 2 * A_HEADS
    vec = lambda n: pl.BlockSpec((1, n), lambda b, h, i: (0, 0))
    c_ops, c_in, c_out, c_shapes = _cast_side_stream(casts, batch * A_HEADS * nq,
                                                     lambda b, h, i: (b * A_HEADS + h) * nq + i)
    return pl.pallas_call(
        functools.partial(_attn_kernel, th=th, dh=dh, lam_init=lam_init, n_cast=len(casts)),
        grid=(batch, A_HEADS, nq),
        in_specs=[pl.BlockSpec(memory_space=pltpu.SMEM),
                  pl.BlockSpec((tq, dv), lambda b, h, i: (b * nq + i, h)),
                  pl.BlockSpec((seq, dv), lambda b, h, i: (b, k_blk0 + h)),
                  pl.BlockSpec((seq, dv), lambda b, h, i: (b, v_blk0 + h)),
                  vec(dv), vec(dh), vec(dh), vec(dh), vec(dh)] + c_in,
        out_specs=[pl.BlockSpec((tq, dv), lambda b, h, i: (b * nq + i, h))] + c_out,
        out_shape=[jax.ShapeDtypeStruct((t, A_HEADS * dv), BF16)] + c_shapes,
        scratch_shapes=[pltpu.VMEM((2, 2, th, LANES), F32), pltpu.VMEM((2, 2, th, LANES), F32),
                        pltpu.VMEM((2, 2, th, dv), F32), pltpu.VMEM((th, th), F32)],
        compiler_params=_params("arbitrary", "arbitrary", "arbitrary"),
        name="diff_attn",
    )(slopes, a_qkv, a_qkv, a_qkv, subln_g.reshape(1, dv),
      lq1.reshape(1, dh), lk1.reshape(1, dh), lq2.reshape(1, dh), lk2.reshape(1, dh), *c_ops)


def _conv_kernel(cb_ref, cc_ref, cx_ref, w_ref, y_ref):
    z = cc_ref[...] * cx_ref[...]
    rows = lax.broadcasted_iota(jnp.int32, z.shape, 0)
    w = w_ref[...]
    y = w[C_KW - 1:C_KW, :] * z
    for back in range(1, C_KW):
        zs = jnp.where(rows >= back, pltpu.roll(z, back, axis=0), 0.0)
        y = y + w[C_KW - 1 - back:C_KW - back, :] * zs
    y_ref[...] = (cb_ref[...] * y).astype(y_ref.dtype)


def _short_conv(c_bcx, conv_w, *, batch, seq, d_model, tc=256):
    cw = d_model // 4
    tc = _tile(cw, tc)
    per = cw // tc
    t = batch * seq
    return pl.pallas_call(
        _conv_kernel,
        grid=(batch, per),
        in_specs=[pl.BlockSpec((seq, tc), lambda b, j: (b, j)),
                  pl.BlockSpec((seq, tc), lambda b, j: (b, per + j)),
                  pl.BlockSpec((seq, tc), lambda b, j: (b, 2 * per + j)),
                  pl.BlockSpec((C_KW, tc), lambda b, j: (0, j))],
        out_specs=pl.BlockSpec((seq, tc), lambda b, j: (b, j)),
        out_shape=jax.ShapeDtypeStruct((t, cw), BF16),
        compiler_params=_params("parallel", "parallel"),
        name="short_conv",
    )(c_bcx, c_bcx, c_bcx, conv_w)


def kernel(x, w_in, m_b_i, m_b_f, a_lq1, a_lk1, a_lq2, a_lk2, a_subln_g, c_conv_w, w_out,
           ln1_g, ln1_b, w_up, w_down, ln2_g, ln2_b):
    batch, seq, d = x.shape
    depth = w_in.shape[0]
    t = batch * seq
    alpha = (2 * depth) ** 0.25
    chunk = _tile(seq, 256)
    nc = seq // chunk

    o_gate = 3 * d // 2
    o_aq = o_gate + N_GATES
    slopes = jnp.asarray([2.0 ** (-8.0 * (i + 1) / A_HEADS) for i in range(A_HEADS)], F32)

    w_in_l = w_in[0].astype(BF16)
    xf = x.reshape(t, d)
    xb = xf.astype(BF16)
    for l in range(depth):
        w_tail = w_in_l[:, o_aq:]
        w_g = jnp.pad(w_in_l[:, o_gate:o_aq], ((0, 0), (0, LANES - N_GATES)))
        m_qkv = _matmul([xb], w_in_l, BF16, name="in_proj_mqkv", n=d)
        m_o = _matmul([xb], w_in_l, F32, name="in_proj_mo", n=d // 2, col0=d)
        a_qkv = _matmul([xb], w_tail, BF16, name="in_proj_aqkv", n=3 * d // 4)
        c_bcx = _matmul([xb], w_tail, F32, name="in_proj_conv", n=3 * d // 4, col0=3 * d // 4)
        gates = _matmul([xb], w_g, F32, name="in_proj_gates", tn=LANES)[:, :N_GATES]
        gates = gates.reshape(batch, nc, chunk, N_GATES).transpose(0, 1, 3, 2)
        bias = jnp.stack([m_b_i[l], m_b_f[l]])

        m_casts = [(w_down, l)] + ([(w_in, l + 1)] if l + 1 < depth else [])
        y_m, w_down_l, *w_in_next = _mlstm(m_qkv, m_o, gates, bias, m_casts,
                                           batch=batch, seq=seq, d_model=d, chunk=chunk)
        lam_init = 0.8 - 0.6 * math.exp(-0.3 * l)
        y_a, w_up_l, w_out_l = _diff_attn(a_qkv, slopes, a_subln_g[l], a_lq1[l], a_lk1[l], a_lq2[l], a_lk2[l],
                                          lam_init, [(w_up, l), (w_out, l)], batch=batch, seq=seq, d_model=d)
        y_c = _short_conv(c_bcx, c_conv_w[l], batch=batch, seq=seq, d_model=d)
        mix = _matmul([y_m, y_a, y_c], w_out_l, F32, name="out_proj")
        xf, xb = _res_ln(xf, mix, ln1_g[l], ln1_b[l], alpha)
        hid = _matmul([xb], w_up_l, BF16, name="mlp_up", relu2=True)
        ff = _matmul_ktiled(hid, w_down_l, name="mlp_down")
        xf, xb = _res_ln(xf, ff, ln2_g[l], ln2_b[l], alpha)
        if w_in_next:
            w_in_l = w_in_next[0]
    return xf.reshape(batch, seq, d)
```

```python
import functools
import math

import jax
import jax.numpy as jnp
from jax import lax
from jax.experimental import pallas as pl
from jax.experimental.pallas import tpu as pltpu

M_HEADS = 4
A_HEADS = 4
C_KW = 3
N_GATES = 2 * M_HEADS
LN_EPS = 1e-5
SUBLN_EPS = 1e-5

V7X_VMEM_LIMIT_BYTES = 56 * 1024 * 1024
LANES = 128
NEG_BIG = -1e30
LOG2E = 1.4426950408889634

F32 = jnp.float32
BF16 = jnp.bfloat16


def _tile(n, want):
    t = min(n, want)
    while n % t:
        t //= 2
    return t


def _params(*sem):
    return pltpu.CompilerParams(dimension_semantics=sem, vmem_limit_bytes=V7X_VMEM_LIMIT_BYTES)


def _lane_tile(x, reps):
    return x if reps == 1 else jnp.concatenate([x] * reps, axis=1)


def _mm_kernel(*refs, relu2):
    x_refs, w_ref, o_ref = refs[:-2], refs[-2], refs[-1]
    acc = None
    k0 = 0
    for x_ref in x_refs:
        kp = x_ref.shape[1]
        part = jnp.dot(x_ref[...], w_ref[k0:k0 + kp, :], preferred_element_type=F32)
        acc = part if acc is None else acc + part
        k0 += kp
    if relu2:
        acc = jnp.square(jnp.maximum(acc, 0.0))
    o_ref[...] = acc.astype(o_ref.dtype)


def _matmul(xs, w, out_dtype, *, name, relu2=False, tm=1024, tn=1024):
    m = xs[0].shape[0]
    k, n = w.shape
    tm, tn = _tile(m, tm), _tile(n, tn)
    assert sum(xp.shape[1] for xp in xs) == k
    return pl.pallas_call(
        functools.partial(_mm_kernel, relu2=relu2),
        grid=(m // tm, n // tn),
        in_specs=[pl.BlockSpec((tm, xp.shape[1]), lambda i, j: (i, 0)) for xp in xs]
        + [pl.BlockSpec((k, tn), lambda i, j: (0, j))],
        out_specs=pl.BlockSpec((tm, tn), lambda i, j: (i, j)),
        out_shape=jax.ShapeDtypeStruct((m, n), out_dtype),
        compiler_params=_params("parallel", "parallel"),
        name=name,
    )(*xs, w)


def _mm_nt_kernel(x_ref, wt_ref, o_ref):
    o_ref[...] = lax.dot_general(x_ref[...], wt_ref[...], (((1,), (1,)), ((), ())),
                                 preferred_element_type=F32).astype(o_ref.dtype)


def _matmul_nt(x, wt, layer, out_dtype, *, name, n=None, row0=0, tm=1024, tn=1024):
    m, k = x.shape
    n = wt.shape[1] if n is None else n
    tm, tn = _tile(m, tm), _tile(n, tn)
    assert row0 % tn == 0 and wt.shape[2] == k
    rb0 = row0 // tn
    return pl.pallas_call(
        _mm_nt_kernel,
        grid=(m // tm, n // tn),
        in_specs=[pl.BlockSpec((tm, k), lambda i, j: (i, 0)),
                  pl.BlockSpec((None, tn, k), lambda i, j: (layer, rb0 + j, 0))],
        out_specs=pl.BlockSpec((tm, tn), lambda i, j: (i, j)),
        out_shape=jax.ShapeDtypeStruct((m, n), out_dtype),
        compiler_params=_params("parallel", "parallel"),
        name=name,
    )(x, wt)


def _mm_acc_kernel(x_ref, w_ref, o_ref):
    @pl.when(pl.program_id(2) == 0)
    def _():
        o_ref[...] = jnp.zeros_like(o_ref)

    o_ref[...] += jnp.dot(x_ref[...], w_ref[...], preferred_element_type=F32)


def _matmul_ktiled(x, w, *, name, tm=2048, tn=1024, tk=1024):
    m, k = x.shape
    n = w.shape[1]
    tm, tn, tk = _tile(m, tm), _tile(n, tn), _tile(k, tk)
    return pl.pallas_call(
        _mm_acc_kernel,
        grid=(m // tm, n // tn, k // tk),
        in_specs=[pl.BlockSpec((tm, tk), lambda i, j, kk: (i, kk)),
                  pl.BlockSpec((tk, tn), lambda i, j, kk: (kk, j))],
        out_specs=pl.BlockSpec((tm, tn), lambda i, j, kk: (i, j)),
        out_shape=jax.ShapeDtypeStruct((m, n), F32),
        compiler_params=_params("parallel", "parallel", "arbitrary"),
        name=name,
    )(x, w)


def _res_ln_kernel(x_ref, y_ref, g_ref, b_ref, o_ref, ob_ref, *, alpha):
    r = alpha * x_ref[...] + y_ref[...]
    mu = jnp.mean(r, axis=-1, keepdims=True)
    d = r - mu
    var = jnp.mean(d * d, axis=-1, keepdims=True)
    o = d * lax.rsqrt(var + LN_EPS) * g_ref[...] + b_ref[...]
    o_ref[...] = o
    ob_ref[...] = o.astype(BF16)


def _res_ln(x, y, g, b, alpha, *, tr=256):
    t, d = x.shape
    tr = _tile(t, tr)
    row = pl.BlockSpec((tr, d), lambda i: (i, 0))
    vec = pl.BlockSpec((1, d), lambda i: (0, 0))
    return pl.pallas_call(
        functools.partial(_res_ln_kernel, alpha=alpha),
        grid=(t // tr,),
        in_specs=[row, row, vec, vec],
        out_specs=[row, row],
        out_shape=[jax.ShapeDtypeStruct((t, d), F32), jax.ShapeDtypeStruct((t, d), BF16)],
        compiler_params=_params("parallel"),
        name="res_ln",
    )(x, y, g.reshape(1, d), b.reshape(1, d))


def _cast_side_stream(casts, n_steps, step_of):
    operands, in_specs, out_specs, out_shapes = [], [], [], []
    for w, layer in casts:
        _, r, c = w.shape
        rows = r // n_steps
        assert rows * n_steps == r and rows % 16 == 0
        operands.append(w)
        in_specs.append(pl.BlockSpec((None, rows, c), lambda *g, layer=layer: (layer, step_of(*g), 0)))
        out_specs.append(pl.BlockSpec((rows, c), lambda *g: (step_of(*g), 0)))
        out_shapes.append(jax.ShapeDtypeStruct((r, c), BF16))
    return operands, in_specs, out_specs, out_shapes


def _round_weights(src_refs, dst_refs):
    for src, dst in zip(src_refs, dst_refs, strict=True):
        dst[...] = src[...].astype(dst.dtype)


def _mlstm_kernel(bias_ref, q_ref, k_ref, v_ref, og_ref, gate_ref, y_ref, c_ref, n_ref, m_ref, *, dqk, dv):
    @pl.when(pl.program_id(1) == 0)
    def _():
        c_ref[...] = jnp.zeros_like(c_ref)
        n_ref[...] = jnp.zeros_like(n_ref)
        m_ref[...] = jnp.zeros_like(m_ref)

    ln = q_ref.shape[0]
    scale = float(dqk) ** -0.5
    rows = lax.broadcasted_iota(jnp.int32, (ln, ln), 0)
    cols = lax.broadcasted_iota(jnp.int32, (ln, ln), 1)
    causal = cols <= rows
    eye = cols == rows

    def to_col(row):
        return jnp.sum(jnp.where(eye, jnp.broadcast_to(row, (ln, ln)), 0.0), axis=1, keepdims=True)

    def to_row(col):
        return jnp.sum(jnp.where(eye, jnp.broadcast_to(col, (ln, ln)), 0.0), axis=0, keepdims=True)

    for head in range(M_HEADS):
        q = q_ref[:, head * dqk:(head + 1) * dqk]
        k = k_ref[:, head * dqk:(head + 1) * dqk]
        v = v_ref[:, head * dv:(head + 1) * dv]
        i_row = gate_ref[head:head + 1, :] + bias_ref[0, head]
        f_row = gate_ref[M_HEADS + head:M_HEADS + head + 1, :] + bias_ref[1, head]
        lf_row = jnp.minimum(f_row, 0.0) - jnp.log(1.0 + jnp.exp(-jnp.abs(f_row)))

        b_col = jnp.sum(jnp.where(causal, jnp.broadcast_to(lf_row, (ln, ln)), 0.0), axis=1, keepdims=True)
        b_row = to_row(b_col)

        m_prev = m_ref[head]
        dmat = jnp.where(causal, b_col - b_row + i_row, -jnp.inf)
        inter = b_col + m_prev
        m_row = jnp.maximum(inter, jnp.max(dmat, axis=1, keepdims=True))
        qk = lax.dot_general(q, k, (((1,), (1,)), ((), ())), preferred_element_type=F32) * scale
        w = qk * jnp.exp(dmat - m_row)
        g = jnp.exp(inter - m_row)

        c_old = c_ref[head]
        n_old = n_ref[head]
        q_c = jnp.dot(q, c_old.astype(BF16), preferred_element_type=F32)
        num = jnp.dot(w.astype(BF16), v, preferred_element_type=F32) + g * q_c
        den = (jnp.sum(w, axis=1, keepdims=True)
               + g * jnp.sum(q.astype(F32) * n_old, axis=1, keepdims=True))
        hid = num * (1.0 / jnp.maximum(jnp.abs(den), jnp.exp(-m_row)))
        gate = 1.0 / (1.0 + jnp.exp(-og_ref[:, head * dv:(head + 1) * dv]))
        y_ref[:, head * dv:(head + 1) * dv] = (gate * hid).astype(y_ref.dtype)

        m_new = m_row[ln - 1:ln, :]
        b_last = b_col[ln - 1:ln, :]
        wl_row = jnp.exp(b_last - b_row + i_row - m_new)
        gl = jnp.exp(b_last + m_prev - m_new)
        wl_col = to_col(wl_row) * scale
        v_w = (wl_col * v.astype(F32)).astype(BF16)
        kv = lax.dot_general(k, v_w, (((0,), (0,)), ((), ())), preferred_element_type=F32)
        c_ref[head] = gl * c_old + kv
        n_ref[head] = gl * n_old + jnp.sum(wl_col * k.astype(F32), axis=0, keepdims=True)
        m_ref[head] = m_new


def _mlstm(m_qkv, m_o, gates, bias, *, batch, seq, d_model, chunk):
    dqk, dv = d_model // 16, d_model // 8
    qw, vw = M_HEADS * dqk, M_HEADS * dv
    nc = seq // chunk
    t = batch * seq
    return pl.pallas_call(
        functools.partial(_mlstm_kernel, dqk=dqk, dv=dv),
        grid=(batch, nc),
        in_specs=[pl.BlockSpec(memory_space=pltpu.SMEM),
                  pl.BlockSpec((chunk, qw), lambda b, c: (b * nc + c, 0)),
                  pl.BlockSpec((chunk, qw), lambda b, c: (b * nc + c, 1)),
                  pl.BlockSpec((chunk, vw), lambda b, c: (b * nc + c, 1)),
                  pl.BlockSpec((chunk, vw), lambda b, c: (b * nc + c, 0)),
                  pl.BlockSpec((None, None, N_GATES, chunk), lambda b, c: (b, c, 0, 0))],
        out_specs=pl.BlockSpec((chunk, vw), lambda b, c: (b * nc + c, 0)),
        out_shape=jax.ShapeDtypeStruct((t, vw), BF16),
        scratch_shapes=[pltpu.VMEM((M_HEADS, dqk, dv), F32), pltpu.VMEM((M_HEADS, 1, dqk), F32),
                        pltpu.VMEM((M_HEADS, 1, 1), F32)],
        compiler_params=_params("parallel", "arbitrary"),
        name="mlstm",
    )(bias, m_qkv, m_qkv, m_qkv, m_o, gates)


def _attn_kernel(*refs, th, dh, lam_init, n_cast):
    slope_ref, q_ref, k_ref, v_ref, g_ref, lq1_ref, lk1_ref, lq2_ref, lk2_ref = refs[:9]
    y_ref = refs[9 + n_cast]
    m_ref, l_ref, acc_ref, bias_ref = refs[10 + 2 * n_cast:]
    _round_weights(refs[9:9 + n_cast], refs[10 + n_cast:10 + 2 * n_cast])

    head = pl.program_id(1)
    qi = pl.program_id(2)
    dv = acc_ref.shape[-1]
    slope2 = slope_ref[head] * LOG2E
    cs = (float(dh) ** -0.5) * LOG2E

    m_ref[...] = jnp.full_like(m_ref, NEG_BIG)
    l_ref[...] = jnp.zeros_like(l_ref)
    acc_ref[...] = jnp.zeros_like(acc_ref)
    rel = (lax.broadcasted_iota(jnp.int32, (th, th), 0)
           - lax.broadcasted_iota(jnp.int32, (th, th), 1)).astype(F32)
    bias_ref[...] = -slope2 * rel

    def half_step(hf, k_blk, v_blk, bias, c0):
        q = q_ref[hf * th:(hf + 1) * th, :]
        for mp in range(2):
            t2 = lax.dot_general(q[:, mp * dh:(mp + 1) * dh], k_blk[:, mp * dh:(mp + 1) * dh],
                                 (((1,), (1,)), ((), ())), preferred_element_type=F32) * cs + bias
            m_old = m_ref[hf, mp]
            m_new = jnp.maximum(m_old, jnp.max(t2, axis=1, keepdims=True) + c0)
            p = jnp.exp2(t2 - _lane_tile(m_new - c0, th // LANES))
            alpha = jnp.exp2(m_old - m_new)
            p_lanes = p[:, :LANES]
            for c in range(1, th // LANES):
                p_lanes = p_lanes + p[:, c * LANES:(c + 1) * LANES]
            l_ref[hf, mp] = alpha * l_ref[hf, mp] + p_lanes
            acc_ref[hf, mp] = (_lane_tile(alpha, dv // LANES) * acc_ref[hf, mp]
                               + jnp.dot(p.astype(BF16), v_blk, preferred_element_type=F32))
            m_ref[hf, mp] = m_new

    def kv_block(kb):
        ks = pl.multiple_of(kb * th, th)
        return k_ref[pl.ds(ks, th), :], v_ref[pl.ds(ks, th), :]

    def offset(n_blocks):
        return -slope2 * (n_blocks * th).astype(F32)

    def body(kb, carry):
        k_blk, v_blk = kv_block(kb)
        bias = bias_ref[...]
        half_step(0, k_blk, v_blk, bias, offset(2 * qi - kb))
        half_step(1, k_blk, v_blk, bias, offset(2 * qi + 1 - kb))
        return carry

    lax.fori_loop(0, 2 * qi, body, 0)
    bias = bias_ref[...]
    diag_bias = jnp.where(rel >= 0.0, bias, NEG_BIG)
    k_blk, v_blk = kv_block(2 * qi)
    half_step(0, k_blk, v_blk, diag_bias, 0.0)
    half_step(1, k_blk, v_blk, bias, -slope2 * float(th))
    k_blk, v_blk = kv_block(2 * qi + 1)
    half_step(1, k_blk, v_blk, diag_bias, 0.0)

    lam = (jnp.exp(jnp.sum(lq1_ref[...] * lk1_ref[...], axis=1, keepdims=True))
           - jnp.exp(jnp.sum(lq2_ref[...] * lk2_ref[...], axis=1, keepdims=True)) + lam_init)
    for hf in range(2):
        inv = [1.0 / jnp.sum(l_ref[hf, mp], axis=1, keepdims=True) for mp in range(2)]
        o = acc_ref[hf, 0] * inv[0] - lam * (acc_ref[hf, 1] * inv[1])
        o = o * lax.rsqrt(jnp.mean(o * o, axis=-1, keepdims=True) + SUBLN_EPS) * g_ref[...]
        y_ref[hf * th:(hf + 1) * th, :] = (o * (1.0 - lam_init)).astype(y_ref.dtype)


def _diff_attn(a_qkv, slopes, subln_g, lq1, lk1, lq2, lk2, lam_init, casts, *, batch, seq, d_model, th=512):
    dh = d_model // 32
    dv = 2 * dh
    th = _tile(seq // 2, th)
    tq = 2 * th
    nq = seq // tq
    t = batch * seq
    k_blk0 = A_HEADS
    v_blk0 = 2 * A_HEADS
    vec = lambda n: pl.BlockSpec((1, n), lambda b, h, i: (0, 0))
    c_ops, c_in, c_out, c_shapes = _cast_side_stream(casts, batch * A_HEADS * nq,
                                                     lambda b, h, i: (b * A_HEADS + h) * nq + i)
    return pl.pallas_call(
        functools.partial(_attn_kernel, th=th, dh=dh, lam_init=lam_init, n_cast=len(casts)),
        grid=(batch, A_HEADS, nq),
        in_specs=[pl.BlockSpec(memory_space=pltpu.SMEM),
                  pl.BlockSpec((tq, dv), lambda b, h, i: (b * nq + i, h)),
                  pl.BlockSpec((seq, dv), lambda b, h, i: (b, k_blk0 + h)),
                  pl.BlockSpec((seq, dv), lambda b, h, i: (b, v_blk0 + h)),
                  vec(dv), vec(dh), vec(dh), vec(dh), vec(dh)] + c_in,
        out_specs=[pl.BlockSpec((tq, dv), lambda b, h, i: (b * nq + i, h))] + c_out,
        out_shape=[jax.ShapeDtypeStruct((t, A_HEADS * dv), BF16)] + c_shapes,
        scratch_shapes=[pltpu.VMEM((2, 2, th, LANES), F32), pltpu.VMEM((2, 2, th, LANES), F32),
                        pltpu.VMEM((2, 2, th, dv), F32), pltpu.VMEM((th, th), F32)],
        compiler_params=_params("arbitrary", "arbitrary", "arbitrary"),
        name="diff_attn",
    )(slopes, a_qkv, a_qkv, a_qkv, subln_g.reshape(1, dv),
      lq1.reshape(1, dh), lk1.reshape(1, dh), lq2.reshape(1, dh), lk2.reshape(1, dh), *c_ops)


def _conv_kernel(cb_ref, cc_ref, cx_ref, w_ref, y_ref):
    z = cc_ref[...] * cx_ref[...]
    rows = lax.broadcasted_iota(jnp.int32, z.shape, 0)
    w = w_ref[...]
    y = w[C_KW - 1:C_KW, :] * z
    for back in range(1, C_KW):
        zs = jnp.where(rows >= back, pltpu.roll(z, back, axis=0), 0.0)
        y = y + w[C_KW - 1 - back:C_KW - back, :] * zs
    y_ref[...] = (cb_ref[...] * y).astype(y_ref.dtype)


def _short_conv(c_bcx, conv_w, *, batch, seq, d_model, tc=256):
    cw = d_model // 4
    tc = _tile(cw, tc)
    per = cw // tc
    t = batch * seq
    return pl.pallas_call(
        _conv_kernel,
        grid=(batch, per),
        in_specs=[pl.BlockSpec((seq, tc), lambda b, j: (b, j)),
                  pl.BlockSpec((seq, tc), lambda b, j: (b, per + j)),
                  pl.BlockSpec((seq, tc), lambda b, j: (b, 2 * per + j)),
                  pl.BlockSpec((C_KW, tc), lambda b, j: (0, j))],
        out_specs=pl.BlockSpec((seq, tc), lambda b, j: (b, j)),
        out_shape=jax.ShapeDtypeStruct((t, cw), BF16),
        compiler_params=_params("parallel", "parallel"),
        name="short_conv",
    )(c_bcx, c_bcx, c_bcx, conv_w)


def kernel(x, w_in, m_b_i, m_b_f, a_lq1, a_lk1, a_lq2, a_lk2, a_subln_g, c_conv_w, w_out,
           ln1_g, ln1_b, w_up, w_down, ln2_g, ln2_b):
    batch, seq, d = x.shape
    depth = w_in.shape[0]
    t = batch * seq
    alpha = (2 * depth) ** 0.25
    chunk = _tile(seq, 256)
    nc = seq // chunk

    o_gate = 3 * d // 2
    o_aq = o_gate + N_GATES
    w_main_t = jnp.swapaxes(w_in, 1, 2).astype(BF16)
    w_tail_t = w_main_t[:, o_aq:, :]
    w_g_t = jnp.pad(w_main_t[:, o_gate:o_aq, :], ((0, 0), (0, LANES - N_GATES), (0, 0)))
    slopes = jnp.asarray([2.0 ** (-8.0 * (i + 1) / A_HEADS) for i in range(A_HEADS)], F32)

    xf = x.reshape(t, d)
    xb = xf.astype(BF16)
    for l in range(depth):
        m_qkv = _matmul_nt(xb, w_main_t, l, BF16, name="in_proj_mqkv", n=d)
        m_o = _matmul_nt(xb, w_main_t, l, F32, name="in_proj_mo", n=d // 2, row0=d)
        a_qkv = _matmul_nt(xb, w_tail_t, l, BF16, name="in_proj_aqkv", n=3 * d // 4)
        c_bcx = _matmul_nt(xb, w_tail_t, l, F32, name="in_proj_conv", n=3 * d // 4, row0=3 * d // 4)
        gates = _matmul_nt(xb, w_g_t, l, F32, name="in_proj_gates", tn=LANES)[:, :N_GATES]
        gates = gates.reshape(batch, nc, chunk, N_GATES).transpose(0, 1, 3, 2)
        bias = jnp.stack([m_b_i[l], m_b_f[l]])

        y_m = _mlstm(m_qkv, m_o, gates, bias, batch=batch, seq=seq, d_model=d, chunk=chunk)
        lam_init = 0.8 - 0.6 * math.exp(-0.3 * l)
        y_a, w_up_l, w_down_l, w_out_l = _diff_attn(
            a_qkv, slopes, a_subln_g[l], a_lq1[l], a_lk1[l], a_lq2[l], a_lk2[l], lam_init,
            [(w_up, l), (w_down, l), (w_out, l)], batch=batch, seq=seq, d_model=d)
        y_c = _short_conv(c_bcx, c_conv_w[l], batch=batch, seq=seq, d_model=d)
        mix = _matmul([y_m, y_a, y_c], w_out_l, F32, name="out_proj")
        xf, xb = _res_ln(xf, mix, ln1_g[l], ln1_b[l], alpha)
        hid = _matmul([xb], w_up_l, BF16, name="mlp_up", relu2=True)
        ff = _matmul_ktiled(hid, w_down_l, name="mlp_down")
        xf, xb = _res_ln(xf, ff, ln2_g[l], ln2_b[l], alpha)
    return xf.reshape(batch, seq, d)
```

```python
import functools
import math

import jax
import jax.numpy as jnp
from jax import lax
from jax.experimental import pallas as pl
from jax.experimental.pallas import tpu as pltpu

M_HEADS = 4
A_HEADS = 4
C_KW = 3
N_GATES = 2 * M_HEADS
LN_EPS = 1e-5
SUBLN_EPS = 1e-5

V7X_VMEM_LIMIT_BYTES = 56 * 1024 * 1024
LANES = 128
NEG_BIG = -1e30
LOG2E = 1.4426950408889634

F32 = jnp.float32
BF16 = jnp.bfloat16


def _tile(n, want):
    t = min(n, want)
    while n % t:
        t //= 2
    return t


def _params(*sem):
    return pltpu.CompilerParams(dimension_semantics=sem, vmem_limit_bytes=V7X_VMEM_LIMIT_BYTES)


def _lane_tile(x, reps):
    return x if reps == 1 else jnp.concatenate([x] * reps, axis=1)


def _mm_kernel(*refs, relu2):
    x_refs, w_ref, o_ref = refs[:-2], refs[-2], refs[-1]
    acc = None
    k0 = 0
    for x_ref in x_refs:
        kp = x_ref.shape[1]
        part = jnp.dot(x_ref[...], w_ref[k0:k0 + kp, :], preferred_element_type=F32)
        acc = part if acc is None else acc + part
        k0 += kp
    if relu2:
        acc = jnp.square(jnp.maximum(acc, 0.0))
    o_ref[...] = acc.astype(o_ref.dtype)


def _matmul(xs, w, out_dtype, *, name, relu2=False, tm=1024, tn=1024):
    m = xs[0].shape[0]
    k, n = w.shape
    tm, tn = _tile(m, tm), _tile(n, tn)
    assert sum(xp.shape[1] for xp in xs) == k
    return pl.pallas_call(
        functools.partial(_mm_kernel, relu2=relu2),
        grid=(m // tm, n // tn),
        in_specs=[pl.BlockSpec((tm, xp.shape[1]), lambda i, j: (i, 0)) for xp in xs]
        + [pl.BlockSpec((k, tn), lambda i, j: (0, j))],
        out_specs=pl.BlockSpec((tm, tn), lambda i, j: (i, j)),
        out_shape=jax.ShapeDtypeStruct((m, n), out_dtype),
        compiler_params=_params("parallel", "parallel"),
        name=name,
    )(*xs, w)


_NT_DIMS = (((1,), (1,)), ((), ()))


def _mm_nt_kernel(x_ref, wt_ref, *rest):
    if len(rest) == 1:
        (o_ref,) = rest
    else:
        ws_ref, o_ref, os_ref = rest

        @pl.when(pl.program_id(1) == 0)
        def _():
            os_ref[...] = lax.dot_general(x_ref[...], ws_ref[...], _NT_DIMS, preferred_element_type=F32)

    o_ref[...] = lax.dot_general(x_ref[...], wt_ref[...], _NT_DIMS, preferred_element_type=F32).astype(o_ref.dtype)


def _matmul_nt(x, wt, layer, out_dtype, *, name, n=None, row0=0, side=None, tm=1024, tn=1024):
    m, k = x.shape
    n = wt.shape[1] if n is None else n
    tm, tn = _tile(m, tm), _tile(n, tn)
    assert row0 % tn == 0 and wt.shape[2] == k
    rb0 = row0 // tn
    operands = [x, wt]
    in_specs = [pl.BlockSpec((tm, k), lambda i, j: (i, 0)),
                pl.BlockSpec((None, tn, k), lambda i, j: (layer, rb0 + j, 0))]
    out_specs = pl.BlockSpec((tm, tn), lambda i, j: (i, j))
    out_shape = jax.ShapeDtypeStruct((m, n), out_dtype)
    if side is not None:
        r = side.shape[1]
        operands.append(side)
        in_specs.append(pl.BlockSpec((None, r, k), lambda i, j: (layer, 0, 0)))
        out_specs = [out_specs, pl.BlockSpec((tm, r), lambda i, j: (i, 0))]
        out_shape = [out_shape, jax.ShapeDtypeStruct((m, r), F32)]
    return pl.pallas_call(
        _mm_nt_kernel,
        grid=(m // tm, n // tn),
        in_specs=in_specs,
        out_specs=out_specs,
        out_shape=out_shape,
        compiler_params=_params("parallel", "arbitrary"),
        name=name,
    )(*operands)


def _mm_acc_kernel(x_ref, w_ref, o_ref):
    @pl.when(pl.program_id(2) == 0)
    def _():
        o_ref[...] = jnp.zeros_like(o_ref)

    o_ref[...] += jnp.dot(x_ref[...], w_ref[...], preferred_element_type=F32)


def _matmul_ktiled(x, w, *, name, tm=2048, tn=1024, tk=2048):
    m, k = x.shape
    n = w.shape[1]
    tm, tn, tk = _tile(m, tm), _tile(n, tn), _tile(k, tk)
    return pl.pallas_call(
        _mm_acc_kernel,
        grid=(m // tm, n // tn, k // tk),
        in_specs=[pl.BlockSpec((tm, tk), lambda i, j, kk: (i, kk)),
                  pl.BlockSpec((tk, tn), lambda i, j, kk: (kk, j))],
        out_specs=pl.BlockSpec((tm, tn), lambda i, j, kk: (i, j)),
        out_shape=jax.ShapeDtypeStruct((m, n), F32),
        compiler_params=_params("parallel", "parallel", "arbitrary"),
        name=name,
    )(x, w)


def _res_ln_kernel(x_ref, y_ref, g_ref, b_ref, o_ref, ob_ref, *, alpha):
    r = alpha * x_ref[...] + y_ref[...]
    mu = jnp.mean(r, axis=-1, keepdims=True)
    d = r - mu
    var = jnp.mean(d * d, axis=-1, keepdims=True)
    o = d * lax.rsqrt(var + LN_EPS) * g_ref[...] + b_ref[...]
    o_ref[...] = o
    ob_ref[...] = o.astype(BF16)


def _res_ln(x, y, g, b, alpha, *, tr=256):
    t, d = x.shape
    tr = _tile(t, tr)
    row = pl.BlockSpec((tr, d), lambda i: (i, 0))
    vec = pl.BlockSpec((1, d), lambda i: (0, 0))
    return pl.pallas_call(
        functools.partial(_res_ln_kernel, alpha=alpha),
        grid=(t // tr,),
        in_specs=[row, row, vec, vec],
        out_specs=[row, row],
        out_shape=[jax.ShapeDtypeStruct((t, d), F32), jax.ShapeDtypeStruct((t, d), BF16)],
        compiler_params=_params("parallel"),
        name="res_ln",
    )(x, y, g.reshape(1, d), b.reshape(1, d))


def _cast_side_stream(casts, n_steps, step_of):
    operands, in_specs, out_specs, out_shapes = [], [], [], []
    for w, layer in casts:
        _, r, c = w.shape
        rows = r // n_steps
        assert rows * n_steps == r and rows % 16 == 0
        operands.append(w)
        in_specs.append(pl.BlockSpec((None, rows, c), lambda *g, layer=layer: (layer, step_of(*g), 0)))
        out_specs.append(pl.BlockSpec((rows, c), lambda *g: (step_of(*g), 0)))
        out_shapes.append(jax.ShapeDtypeStruct((r, c), BF16))
    return operands, in_specs, out_specs, out_shapes


def _round_weights(src_refs, dst_refs):
    for src, dst in zip(src_refs, dst_refs, strict=True):
        dst[...] = src[...].astype(dst.dtype)


def _mlstm_kernel(bias_ref, q_ref, k_ref, v_ref, og_ref, gate_ref, y_ref, c_ref, n_ref, m_ref, *, dqk, dv):
    @pl.when(pl.program_id(1) == 0)
    def _():
        c_ref[...] = jnp.zeros_like(c_ref)
        n_ref[...] = jnp.zeros_like(n_ref)
        m_ref[...] = jnp.zeros_like(m_ref)

    ln = q_ref.shape[0]
    scale = float(dqk) ** -0.5
    rows = lax.broadcasted_iota(jnp.int32, (ln, ln), 0)
    cols = lax.broadcasted_iota(jnp.int32, (ln, ln), 1)
    causal = cols <= rows
    eye = cols == rows

    def to_col(row):
        return jnp.sum(jnp.where(eye, jnp.broadcast_to(row, (ln, ln)), 0.0), axis=1, keepdims=True)

    def to_row(col):
        return jnp.sum(jnp.where(eye, jnp.broadcast_to(col, (ln, ln)), 0.0), axis=0, keepdims=True)

    for head in range(M_HEADS):
        q = q_ref[:, head * dqk:(head + 1) * dqk]
        k = k_ref[:, head * dqk:(head + 1) * dqk]
        v = v_ref[:, head * dv:(head + 1) * dv]
        i_row = gate_ref[head:head + 1, :] + bias_ref[0, head]
        f_row = gate_ref[M_HEADS + head:M_HEADS + head + 1, :] + bias_ref[1, head]
        lf_row = jnp.minimum(f_row, 0.0) - jnp.log(1.0 + jnp.exp(-jnp.abs(f_row)))

        b_col = jnp.sum(jnp.where(causal, jnp.broadcast_to(lf_row, (ln, ln)), 0.0), axis=1, keepdims=True)
        b_row = to_row(b_col)

        m_prev = m_ref[head]
        dmat = jnp.where(causal, b_col - b_row + i_row, -jnp.inf)
        inter = b_col + m_prev
        m_row = jnp.maximum(inter, jnp.max(dmat, axis=1, keepdims=True))
        qk = lax.dot_general(q, k, (((1,), (1,)), ((), ())), preferred_element_type=F32) * scale
        w = qk * jnp.exp(dmat - m_row)
        g = jnp.exp(inter - m_row)

        c_old = c_ref[head]
        n_old = n_ref[head]
        q_c = jnp.dot(q, c_old.astype(BF16), preferred_element_type=F32)
        num = jnp.dot(w.astype(BF16), v, preferred_element_type=F32) + g * q_c
        den = (jnp.sum(w, axis=1, keepdims=True)
               + g * jnp.sum(q.astype(F32) * n_old, axis=1, keepdims=True))
        hid = num * (1.0 / jnp.maximum(jnp.abs(den), jnp.exp(-m_row)))
        gate = 1.0 / (1.0 + jnp.exp(-og_ref[:, head * dv:(head + 1) * dv]))
        y_ref[:, head * dv:(head + 1) * dv] = (gate * hid).astype(y_ref.dtype)

        m_new = m_row[ln - 1:ln, :]
        b_last = b_col[ln - 1:ln, :]
        wl_row = jnp.exp(b_last - b_row + i_row - m_new)
        gl = jnp.exp(b_last + m_prev - m_new)
        wl_col = to_col(wl_row) * scale
        v_w = (wl_col * v.astype(F32)).astype(BF16)
        kv = lax.dot_general(k, v_w, (((0,), (0,)), ((), ())), preferred_element_type=F32)
        c_ref[head] = gl * c_old + kv
        n_ref[head] = gl * n_old + jnp.sum(wl_col * k.astype(F32), axis=0, keepdims=True)
        m_ref[head] = m_new


def _mlstm(m_qkv, m_o, gates, bias, *, batch, seq, d_model, chunk):
    dqk, dv = d_model // 16, d_model // 8
    qw, vw = M_HEADS * dqk, M_HEADS * dv
    nc = seq // chunk
    t = batch * seq
    return pl.pallas_call(
        functools.partial(_mlstm_kernel, dqk=dqk, dv=dv),
        grid=(batch, nc),
        in_specs=[pl.BlockSpec(memory_space=pltpu.SMEM),
                  pl.BlockSpec((chunk, qw), lambda b, c: (b * nc + c, 0)),
                  pl.BlockSpec((chunk, qw), lambda b, c: (b * nc + c, 1)),
                  pl.BlockSpec((chunk, vw), lambda b, c: (b * nc + c, 1)),
                  pl.BlockSpec((chunk, vw), lambda b, c: (b * nc + c, 0)),
                  pl.BlockSpec((None, None, N_GATES, chunk), lambda b, c: (b, c, 0, 0))],
        out_specs=pl.BlockSpec((chunk, vw), lambda b, c: (b * nc + c, 0)),
        out_shape=jax.ShapeDtypeStruct((t, vw), BF16),
        scratch_shapes=[pltpu.VMEM((M_HEADS, dqk, dv), F32), pltpu.VMEM((M_HEADS, 1, dqk), F32),
                        pltpu.VMEM((M_HEADS, 1, 1), F32)],
        compiler_params=_params("parallel", "arbitrary"),
        name="mlstm",
    )(bias, m_qkv, m_qkv, m_qkv, m_o, gates)


def _attn_kernel(*refs, th, dh, lam_init, n_cast):
    slope_ref, q_ref, k_ref, v_ref, g_ref, lq1_ref, lk1_ref, lq2_ref, lk2_ref = refs[:9]
    y_ref = refs[9 + n_cast]
    m_ref, l_ref, acc_ref, rowt_ref = refs[10 + 2 * n_cast:]
    _round_weights(refs[9:9 + n_cast], refs[10 + n_cast:10 + 2 * n_cast])

    head = pl.program_id(1)
    qi = pl.program_id(2)
    dv = acc_ref.shape[-1]
    slope2 = slope_ref[head] * LOG2E
    cs = (float(dh) ** -0.5) * LOG2E

    m_ref[...] = jnp.full_like(m_ref, NEG_BIG)
    l_ref[...] = jnp.zeros_like(l_ref)
    acc_ref[...] = jnp.zeros_like(acc_ref)
    col_bias = slope2 * lax.broadcasted_iota(jnp.int32, (1, th), 1).astype(F32)
    rowt_ref[...] = slope2 * lax.broadcasted_iota(jnp.int32, (th, LANES), 0).astype(F32)

    def half_step(hf, k_blk, v_blk, causal, c0):
        q = q_ref[hf * th:(hf + 1) * th, :]
        row_off = c0 - rowt_ref[...]
        for mp in range(2):
            t2 = lax.dot_general(q[:, mp * dh:(mp + 1) * dh], k_blk[:, mp * dh:(mp + 1) * dh],
                                 (((1,), (1,)), ((), ())), preferred_element_type=F32) * cs + col_bias
            if causal is not None:
                t2 = jnp.where(causal, t2, NEG_BIG)
            m_old = m_ref[hf, mp]
            m_new = jnp.maximum(m_old, jnp.max(t2, axis=1, keepdims=True) + row_off)
            p = jnp.exp2(t2 - _lane_tile(m_new - row_off, th // LANES))
            alpha = jnp.exp2(m_old - m_new)
            p_lanes = p[:, :LANES]
            for c in range(1, th // LANES):
                p_lanes = p_lanes + p[:, c * LANES:(c + 1) * LANES]
            l_ref[hf, mp] = alpha * l_ref[hf, mp] + p_lanes
            acc_ref[hf, mp] = (_lane_tile(alpha, dv // LANES) * acc_ref[hf, mp]
                               + jnp.dot(p.astype(BF16), v_blk, preferred_element_type=F32))
            m_ref[hf, mp] = m_new

    def kv_block(kb):
        ks = pl.multiple_of(kb * th, th)
        return k_ref[pl.ds(ks, th), :], v_ref[pl.ds(ks, th), :]

    def offset(n_blocks):
        return -slope2 * (n_blocks * th).astype(F32)

    def body(kb, carry):
        k_blk, v_blk = kv_block(kb)
        half_step(0, k_blk, v_blk, None, offset(2 * qi - kb))
        half_step(1, k_blk, v_blk, None, offset(2 * qi + 1 - kb))
        return carry

    lax.fori_loop(0, 2 * qi, body, 0)
    causal = (lax.broadcasted_iota(jnp.int32, (th, th), 0) >= lax.broadcasted_iota(jnp.int32, (th, th), 1))
    k_blk, v_blk = kv_block(2 * qi)
    half_step(0, k_blk, v_blk, causal, 0.0)
    half_step(1, k_blk, v_blk, None, -slope2 * float(th))
    k_blk, v_blk = kv_block(2 * qi + 1)
    half_step(1, k_blk, v_blk, causal, 0.0)

    lam = (jnp.exp(jnp.sum(lq1_ref[...] * lk1_ref[...], axis=1, keepdims=True))
           - jnp.exp(jnp.sum(lq2_ref[...] * lk2_ref[...], axis=1, keepdims=True)) + lam_init)
    for hf in range(2):
        inv = [1.0 / jnp.sum(l_ref[hf, mp], axis=1, keepdims=True) for mp in range(2)]
        o = acc_ref[hf, 0] * inv[0] - lam * (acc_ref[hf, 1] * inv[1])
        o = o * lax.rsqrt(jnp.mean(o * o, axis=-1, keepdims=True) + SUBLN_EPS) * g_ref[...]
        y_ref[hf * th:(hf + 1) * th, :] = (o * (1.0 - lam_init)).astype(y_ref.dtype)


def _diff_attn(a_qkv, slopes, subln_g, lq1, lk1, lq2, lk2, lam_init, casts, *, batch, seq, d_model, th=512):
    dh = d_model // 32
    dv = 2 * dh
    th = _tile(seq // 2, th)
    tq = 2 * th
    nq = seq // tq
    t = batch * seq
    k_blk0 = A_HEADS
    v_blk0 = 2 * A_HEADS
    vec = lambda n: pl.BlockSpec((1, n), lambda b, h, i: (0, 0))
    c_ops, c_in, c_out, c_shapes = _cast_side_stream(casts, batch * A_HEADS * nq,
                                                     lambda b, h, i: (b * A_HEADS + h) * nq + i)
    return pl.pallas_call(
        functools.partial(_attn_kernel, th=th, dh=dh, lam_init=lam_init, n_cast=len(casts)),
        grid=(batch, A_HEADS, nq),
        in_specs=[pl.BlockSpec(memory_space=pltpu.SMEM),
                  pl.BlockSpec((tq, dv), lambda b, h, i: (b * nq + i, h)),
                  pl.BlockSpec((seq, dv), lambda b, h, i: (b, k_blk0 + h)),
                  pl.BlockSpec((seq, dv), lambda b, h, i: (b, v_blk0 + h)),
                  vec(dv), vec(dh), vec(dh), vec(dh), vec(dh)] + c_in,
        out_specs=[pl.BlockSpec((tq, dv), lambda b, h, i: (b * nq + i, h))] + c_out,
        out_shape=[jax.ShapeDtypeStruct((t, A_HEADS * dv), BF16)] + c_shapes,
        scratch_shapes=[pltpu.VMEM((2, 2, th, LANES), F32), pltpu.VMEM((2, 2, th, LANES), F32),
                        pltpu.VMEM((2, 2, th, dv), F32), pltpu.VMEM((th, LANES), F32)],
        compiler_params=_params("arbitrary", "arbitrary", "arbitrary"),
        name="diff_attn",
    )(slopes, a_qkv, a_qkv, a_qkv, subln_g.reshape(1, dv),
      lq1.reshape(1, dh), lk1.reshape(1, dh), lq2.reshape(1, dh), lk2.reshape(1, dh), *c_ops)


def _conv_kernel(cb_ref, cc_ref, cx_ref, w_ref, y_ref):
    z = cc_ref[...] * cx_ref[...]
    rows = lax.broadcasted_iota(jnp.int32, z.shape, 0)
    w = w_ref[...]
    y = w[C_KW - 1:C_KW, :] * z
    for back in range(1, C_KW):
        zs = jnp.where(rows >= back, pltpu.roll(z, back, axis=0), 0.0)
        y = y + w[C_KW - 1 - back:C_KW - back, :] * zs
    y_ref[...] = (cb_ref[...] * y).astype(y_ref.dtype)


def _short_conv(c_bcx, conv_w, *, batch, seq, d_model, tc=256):
    cw = d_model // 4
    tc = _tile(cw, tc)
    per = cw // tc
    t = batch * seq
    return pl.pallas_call(
        _conv_kernel,
        grid=(batch, per),
        in_specs=[pl.BlockSpec((seq, tc), lambda b, j: (b, j)),
                  pl.BlockSpec((seq, tc), lambda b, j: (b, per + j)),
                  pl.BlockSpec((seq, tc), lambda b, j: (b, 2 * per + j)),
                  pl.BlockSpec((C_KW, tc), lambda b, j: (0, j))],
        out_specs=pl.BlockSpec((seq, tc), lambda b, j: (b, j)),
        out_shape=jax.ShapeDtypeStruct((t, cw), BF16),
        compiler_params=_params("parallel", "parallel"),
        name="short_conv",
    )(c_bcx, c_bcx, c_bcx, conv_w)


def kernel(x, w_in, m_b_i, m_b_f, a_lq1, a_lk1, a_lq2, a_lk2, a_subln_g, c_conv_w, w_out,
           ln1_g, ln1_b, w_up, w_down, ln2_g, ln2_b):
    batch, seq, d = x.shape
    depth = w_in.shape[0]
    t = batch * seq
    alpha = (2 * depth) ** 0.25
    chunk = _tile(seq, 256)
    nc = seq // chunk

    o_gate = 3 * d // 2
    o_aq = o_gate + N_GATES
    w_main_t = jnp.swapaxes(w_in, 1, 2).astype(BF16)
    w_tail_t = w_main_t[:, o_aq:, :]
    w_g_t = jnp.pad(w_main_t[:, o_gate:o_aq, :], ((0, 0), (0, LANES - N_GATES), (0, 0)))
    slopes = jnp.asarray([2.0 ** (-8.0 * (i + 1) / A_HEADS) for i in range(A_HEADS)], F32)

    xf = x.reshape(t, d)
    xb = xf.astype(BF16)
    for l in range(depth):
        m_qkv, gates = _matmul_nt(xb, w_main_t, l, BF16, name="in_proj_mqkv", n=d, side=w_g_t)
        m_o = _matmul_nt(xb, w_main_t, l, F32, name="in_proj_mo", n=d // 2, row0=d)
        a_qkv = _matmul_nt(xb, w_tail_t, l, BF16, name="in_proj_aqkv", n=3 * d // 4)
        c_bcx = _matmul_nt(xb, w_tail_t, l, F32, name="in_proj_conv", n=3 * d // 4, row0=3 * d // 4)
        gates = gates[:, :N_GATES].reshape(batch, nc, chunk, N_GATES).transpose(0, 1, 3, 2)
        bias = jnp.stack([m_b_i[l], m_b_f[l]])

        y_m = _mlstm(m_qkv, m_o, gates, bias, batch=batch, seq=seq, d_model=d, chunk=chunk)
        lam_init = 0.8 - 0.6 * math.exp(-0.3 * l)
        y_a, w_up_l, w_down_l, w_out_l = _diff_attn(
            a_qkv, slopes, a_subln_g[l], a_lq1[l], a_lk1[l], a_lq2[l], a_lk2[l], lam_init,
            [(w_up, l), (w_down, l), (w_out, l)], batch=batch, seq=seq, d_model=d)
        y_c = _short_conv(c_bcx, c_conv_w[l], batch=batch, seq=seq, d_model=d)
        mix = _matmul([y_m, y_a, y_c], w_out_l, F32, name="out_proj")
        xf, xb = _res_ln(xf, mix, ln1_g[l], ln1_b[l], alpha)
        hid = _matmul([xb], w_up_l, BF16, name="mlp_up", relu2=True)
        ff = _matmul_ktiled(hid, w_down_l, name="mlp_down")
        xf, xb = _res_ln(xf, ff, ln2_g[l], ln2_b[l], alpha)
    return xf.reshape(batch, seq, d)
```

```python
import functools
import math

import jax
import jax.numpy as jnp
from jax import lax
from jax.experimental import pallas as pl
from jax.experimental.pallas import tpu as pltpu

M_HEADS = 4
A_HEADS = 4
C_KW = 3
N_GATES = 2 * M_HEADS
LN_EPS = 1e-5
SUBLN_EPS = 1e-5

V7X_VMEM_LIMIT_BYTES = 56 * 1024 * 1024
LANES = 128
NEG_BIG = -1e30
LOG2E = 1.4426950408889634

F32 = jnp.float32
BF16 = jnp.bfloat16


def _tile(n, want):
    t = min(n, want)
    while n % t:
        t //= 2
    return t


def _params(*sem):
    return pltpu.CompilerParams(dimension_semantics=sem, vmem_limit_bytes=V7X_VMEM_LIMIT_BYTES)


def _lane_tile(x, reps):
    return x if reps == 1 else jnp.concatenate([x] * reps, axis=1)


def _mm_kernel(*refs, relu2):
    x_refs, w_ref, o_ref = refs[:-2], refs[-2], refs[-1]
    acc = None
    k0 = 0
    for x_ref in x_refs:
        kp = x_ref.shape[1]
        part = jnp.dot(x_ref[...], w_ref[k0:k0 + kp, :], preferred_element_type=F32)
        acc = part if acc is None else acc + part
        k0 += kp
    if relu2:
        acc = jnp.square(jnp.maximum(acc, 0.0))
    o_ref[...] = acc.astype(o_ref.dtype)


def _matmul(xs, w, out_dtype, *, name, relu2=False, tm=1024, tn=1024):
    m = xs[0].shape[0]
    k, n = w.shape
    tm, tn = _tile(m, tm), _tile(n, tn)
    assert sum(xp.shape[1] for xp in xs) == k
    return pl.pallas_call(
        functools.partial(_mm_kernel, relu2=relu2),
        grid=(m // tm, n // tn),
        in_specs=[pl.BlockSpec((tm, xp.shape[1]), lambda i, j: (i, 0)) for xp in xs]
        + [pl.BlockSpec((k, tn), lambda i, j: (0, j))],
        out_specs=pl.BlockSpec((tm, tn), lambda i, j: (i, j)),
        out_shape=jax.ShapeDtypeStruct((m, n), out_dtype),
        compiler_params=_params("parallel", "parallel"),
        name=name,
    )(*xs, w)


_NT_DIMS = (((1,), (1,)), ((), ()))


def _mm_nt_kernel(x_ref, wt_ref, *rest):
    if len(rest) == 1:
        (o_ref,) = rest
    else:
        ws_ref, o_ref, os_ref = rest

        @pl.when(pl.program_id(1) == 0)
        def _():
            os_ref[...] = lax.dot_general(x_ref[...], ws_ref[...], _NT_DIMS, preferred_element_type=F32)

    o_ref[...] = lax.dot_general(x_ref[...], wt_ref[...], _NT_DIMS, preferred_element_type=F32).astype(o_ref.dtype)


def _matmul_nt(x, wt, layer, out_dtype, *, name, n, row0=0, side=None, tm=1024, tn=1024):
    m, k = x.shape
    tm, tn = _tile(m, tm), _tile(n, tn)
    assert row0 % tn == 0 and wt.shape[2] == k
    rb0 = row0 // tn
    operands = [x, wt]
    in_specs = [pl.BlockSpec((tm, k), lambda i, j: (i, 0)),
                pl.BlockSpec((None, tn, k), lambda i, j: (layer, rb0 + j, 0))]
    out_specs = pl.BlockSpec((tm, tn), lambda i, j: (i, j))
    out_shape = jax.ShapeDtypeStruct((m, n), out_dtype)
    if side is not None:
        r = side.shape[1]
        operands.append(side)
        in_specs.append(pl.BlockSpec((None, r, k), lambda i, j: (layer, 0, 0)))
        out_specs = [out_specs, pl.BlockSpec((tm, r), lambda i, j: (i, 0))]
        out_shape = [out_shape, jax.ShapeDtypeStruct((m, r), F32)]
    return pl.pallas_call(
        _mm_nt_kernel,
        grid=(m // tm, n // tn),
        in_specs=in_specs,
        out_specs=out_specs,
        out_shape=out_shape,
        compiler_params=_params("parallel", "arbitrary"),
        name=name,
    )(*operands)


def _mm_acc_kernel(x_ref, w_ref, o_ref):
    @pl.when(pl.program_id(2) == 0)
    def _():
        o_ref[...] = jnp.zeros_like(o_ref)

    o_ref[...] += jnp.dot(x_ref[...], w_ref[...], preferred_element_type=F32)


def _matmul_ktiled(x, w, *, name, tm=2048, tn=1024, tk=2048):
    m, k = x.shape
    n = w.shape[1]
    tm, tn, tk = _tile(m, tm), _tile(n, tn), _tile(k, tk)
    return pl.pallas_call(
        _mm_acc_kernel,
        grid=(m // tm, n // tn, k // tk),
        in_specs=[pl.BlockSpec((tm, tk), lambda i, j, kk: (i, kk)),
                  pl.BlockSpec((tk, tn), lambda i, j, kk: (kk, j))],
        out_specs=pl.BlockSpec((tm, tn), lambda i, j, kk: (i, j)),
        out_shape=jax.ShapeDtypeStruct((m, n), F32),
        compiler_params=_params("parallel", "parallel", "arbitrary"),
        name=name,
    )(x, w)


def _res_ln_kernel(x_ref, y_ref, g_ref, b_ref, o_ref, ob_ref, *, alpha):
    r = alpha * x_ref[...] + y_ref[...]
    mu = jnp.mean(r, axis=-1, keepdims=True)
    d = r - mu
    var = jnp.mean(d * d, axis=-1, keepdims=True)
    o = d * lax.rsqrt(var + LN_EPS) * g_ref[...] + b_ref[...]
    o_ref[...] = o
    ob_ref[...] = o.astype(BF16)


def _res_ln(x, y, g, b, alpha, *, tr=256):
    t, d = x.shape
    tr = _tile(t, tr)
    row = pl.BlockSpec((tr, d), lambda i: (i, 0))
    vec = pl.BlockSpec((1, d), lambda i: (0, 0))
    return pl.pallas_call(
        functools.partial(_res_ln_kernel, alpha=alpha),
        grid=(t // tr,),
        in_specs=[row, row, vec, vec],
        out_specs=[row, row],
        out_shape=[jax.ShapeDtypeStruct((t, d), F32), jax.ShapeDtypeStruct((t, d), BF16)],
        compiler_params=_params("parallel"),
        name="res_ln",
    )(x, y, g.reshape(1, d), b.reshape(1, d))


def _cast_side_stream(casts, n_steps, step_of):
    operands, in_specs, out_specs, out_shapes = [], [], [], []
    for w, layer in casts:
        _, r, c = w.shape
        rows = r // n_steps
        assert rows * n_steps == r and rows % 16 == 0
        operands.append(w)
        in_specs.append(pl.BlockSpec((None, rows, c), lambda *g, layer=layer: (layer, step_of(*g), 0)))
        out_specs.append(pl.BlockSpec((rows, c), lambda *g: (step_of(*g), 0)))
        out_shapes.append(jax.ShapeDtypeStruct((r, c), BF16))
    return operands, in_specs, out_specs, out_shapes


def _round_weights(src_refs, dst_refs):
    for src, dst in zip(src_refs, dst_refs, strict=True):
        dst[...] = src[...].astype(dst.dtype)


def _mlstm_kernel(bias_ref, q_ref, k_ref, v_ref, og_ref, gate_ref, y_ref, c_ref, n_ref, m_ref, *, dqk, dv):
    @pl.when(pl.program_id(1) == 0)
    def _():
        c_ref[...] = jnp.zeros_like(c_ref)
        n_ref[...] = jnp.zeros_like(n_ref)
        m_ref[...] = jnp.zeros_like(m_ref)

    ln = q_ref.shape[0]
    scale = float(dqk) ** -0.5
    rows = lax.broadcasted_iota(jnp.int32, (ln, ln), 0)
    cols = lax.broadcasted_iota(jnp.int32, (ln, ln), 1)
    causal = cols <= rows
    eye = cols == rows

    def to_col(row):
        return jnp.sum(jnp.where(eye, jnp.broadcast_to(row, (ln, ln)), 0.0), axis=1, keepdims=True)

    def to_row(col):
        return jnp.sum(jnp.where(eye, jnp.broadcast_to(col, (ln, ln)), 0.0), axis=0, keepdims=True)

    for head in range(M_HEADS):
        q = q_ref[:, head * dqk:(head + 1) * dqk]
        k = k_ref[:, head * dqk:(head + 1) * dqk]
        v = v_ref[:, head * dv:(head + 1) * dv]
        i_row = gate_ref[head:head + 1, :] + bias_ref[0, head]
        f_row = gate_ref[M_HEADS + head:M_HEADS + head + 1, :] + bias_ref[1, head]
        lf_row = jnp.minimum(f_row, 0.0) - jnp.log(1.0 + jnp.exp(-jnp.abs(f_row)))

        b_col = jnp.sum(jnp.where(causal, jnp.broadcast_to(lf_row, (ln, ln)), 0.0), axis=1, keepdims=True)
        b_row = to_row(b_col)

        m_prev = m_ref[head]
        dmat = jnp.where(causal, b_col - b_row + i_row, -jnp.inf)
        inter = b_col + m_prev
        m_row = jnp.maximum(inter, jnp.max(dmat, axis=1, keepdims=True))
        qk = lax.dot_general(q, k, _NT_DIMS, preferred_element_type=F32) * scale
        w = qk * jnp.exp(dmat - m_row)
        g = jnp.exp(inter - m_row)

        c_old = c_ref[head]
        n_old = n_ref[head]
        q_c = jnp.dot(q, c_old.astype(BF16), preferred_element_type=F32)
        num = jnp.dot(w.astype(BF16), v, preferred_element_type=F32) + g * q_c
        den = (jnp.sum(w, axis=1, keepdims=True)
               + g * jnp.sum(q.astype(F32) * n_old, axis=1, keepdims=True))
        hid = num * (1.0 / jnp.maximum(jnp.abs(den), jnp.exp(-m_row)))
        gate = 1.0 / (1.0 + jnp.exp(-og_ref[:, head * dv:(head + 1) * dv]))
        y_ref[:, head * dv:(head + 1) * dv] = (gate * hid).astype(y_ref.dtype)

        m_new = m_row[ln - 1:ln, :]
        b_last = b_col[ln - 1:ln, :]
        wl_row = jnp.exp(b_last - b_row + i_row - m_new)
        gl = jnp.exp(b_last + m_prev - m_new)
        wl_col = to_col(wl_row) * scale
        v_w = (wl_col * v.astype(F32)).astype(BF16)
        kv = lax.dot_general(k, v_w, (((0,), (0,)), ((), ())), preferred_element_type=F32)
        c_ref[head] = gl * c_old + kv
        n_ref[head] = gl * n_old + jnp.sum(wl_col * k.astype(F32), axis=0, keepdims=True)
        m_ref[head] = m_new


def _mlstm(m_qkv, m_o, gates, bias, *, batch, seq, d_model, chunk):
    dqk, dv = d_model // 16, d_model // 8
    qw, vw = M_HEADS * dqk, M_HEADS * dv
    nc = seq // chunk
    t = batch * seq
    return pl.pallas_call(
        functools.partial(_mlstm_kernel, dqk=dqk, dv=dv),
        grid=(batch, nc),
        in_specs=[pl.BlockSpec(memory_space=pltpu.SMEM),
                  pl.BlockSpec((chunk, qw), lambda b, c: (b * nc + c, 0)),
                  pl.BlockSpec((chunk, qw), lambda b, c: (b * nc + c, 1)),
                  pl.BlockSpec((chunk, vw), lambda b, c: (b * nc + c, 1)),
                  pl.BlockSpec((chunk, vw), lambda b, c: (b * nc + c, 0)),
                  pl.BlockSpec((None, None, N_GATES, chunk), lambda b, c: (b, c, 0, 0))],
        out_specs=pl.BlockSpec((chunk, vw), lambda b, c: (b * nc + c, 0)),
        out_shape=jax.ShapeDtypeStruct((t, vw), BF16),
        scratch_shapes=[pltpu.VMEM((M_HEADS, dqk, dv), F32), pltpu.VMEM((M_HEADS, 1, dqk), F32),
                        pltpu.VMEM((M_HEADS, 1, 1), F32)],
        compiler_params=_params("parallel", "arbitrary"),
        name="mlstm",
    )(bias, m_qkv, m_qkv, m_qkv, m_o, gates)


def _attn_kernel(*refs, th, dh, lam_init, n_cast):
    slope_ref, q_ref, k_ref, v_ref, g_ref, lq1_ref, lk1_ref, lq2_ref, lk2_ref = refs[:9]
    y_ref = refs[9 + n_cast]
    m_ref, l_ref, acc_ref, rowt_ref = refs[10 + 2 * n_cast:]
    _round_weights(refs[9:9 + n_cast], refs[10 + n_cast:10 + 2 * n_cast])

    head = pl.program_id(1)
    qi = pl.program_id(2)
    dv = acc_ref.shape[-1]
    slope2 = slope_ref[head] * LOG2E
    cs = (float(dh) ** -0.5) * LOG2E

    m_ref[...] = jnp.full_like(m_ref, NEG_BIG)
    l_ref[...] = jnp.zeros_like(l_ref)
    acc_ref[...] = jnp.zeros_like(acc_ref)
    col_bias = slope2 * lax.broadcasted_iota(jnp.int32, (1, th), 1).astype(F32)
    rowt_ref[...] = slope2 * lax.broadcasted_iota(jnp.int32, (th, LANES), 0).astype(F32)

    def half_step(hf, k_blk, v_blk, causal, c0):
        q = q_ref[hf * th:(hf + 1) * th, :]
        row_off = c0 - rowt_ref[...]
        for mp in range(2):
            t2 = lax.dot_general(q[:, mp * dh:(mp + 1) * dh], k_blk[:, mp * dh:(mp + 1) * dh],
                                 _NT_DIMS, preferred_element_type=F32) * cs + col_bias
            if causal is not None:
                t2 = jnp.where(causal, t2, NEG_BIG)
            m_old = m_ref[hf, mp]
            m_new = jnp.maximum(m_old, jnp.max(t2, axis=1, keepdims=True) + row_off)
            p = jnp.exp2(t2 - _lane_tile(m_new - row_off, th // LANES))
            alpha = jnp.exp2(m_old - m_new)
            p_lanes = p[:, :LANES]
            for c in range(1, th // LANES):
                p_lanes = p_lanes + p[:, c * LANES:(c + 1) * LANES]
            l_ref[hf, mp] = alpha * l_ref[hf, mp] + p_lanes
            acc_ref[hf, mp] = (_lane_tile(alpha, dv // LANES) * acc_ref[hf, mp]
                               + jnp.dot(p.astype(BF16), v_blk, preferred_element_type=F32))
            m_ref[hf, mp] = m_new

    def kv_block(kb):
        ks = pl.multiple_of(kb * th, th)
        return k_ref[pl.ds(ks, th), :], v_ref[pl.ds(ks, th), :]

    def offset(n_blocks):
        return -slope2 * (n_blocks * th).astype(F32)

    def body(kb, carry):
        k_blk, v_blk = kv_block(kb)
        half_step(0, k_blk, v_blk, None, offset(2 * qi - kb))
        half_step(1, k_blk, v_blk, None, offset(2 * qi + 1 - kb))
        return carry

    lax.fori_loop(0, 2 * qi, body, 0)
    causal = (lax.broadcasted_iota(jnp.int32, (th, th), 0) >= lax.broadcasted_iota(jnp.int32, (th, th), 1))
    k_blk, v_blk = kv_block(2 * qi)
    half_step(0, k_blk, v_blk, causal, 0.0)
    half_step(1, k_blk, v_blk, None, -slope2 * float(th))
    k_blk, v_blk = kv_block(2 * qi + 1)
    half_step(1, k_blk, v_blk, causal, 0.0)

    lam = (jnp.exp(jnp.sum(lq1_ref[...] * lk1_ref[...], axis=1, keepdims=True))
           - jnp.exp(jnp.sum(lq2_ref[...] * lk2_ref[...], axis=1, keepdims=True)) + lam_init)
    for hf in range(2):
        inv = [1.0 / jnp.sum(l_ref[hf, mp], axis=1, keepdims=True) for mp in range(2)]
        o = acc_ref[hf, 0] * inv[0] - lam * (acc_ref[hf, 1] * inv[1])
        o = o * lax.rsqrt(jnp.mean(o * o, axis=-1, keepdims=True) + SUBLN_EPS) * g_ref[...]
        y_ref[hf * th:(hf + 1) * th, :] = (o * (1.0 - lam_init)).astype(y_ref.dtype)


def _diff_attn(a_qkv, slopes, subln_g, lq1, lk1, lq2, lk2, lam_init, casts, *, batch, seq, d_model, th=512):
    dh = d_model // 32
    dv = 2 * dh
    th = _tile(seq // 2, th)
    tq = 2 * th
    nq = seq // tq
    t = batch * seq
    k_blk0 = A_HEADS
    v_blk0 = 2 * A_HEADS
    vec = lambda n: pl.BlockSpec((1, n), lambda b, h, i: (0, 0))
    c_ops, c_in, c_out, c_shapes = _cast_side_stream(casts, batch * A_HEADS * nq,
                                                     lambda b, h, i: (b * A_HEADS + h) * nq + i)
    return pl.pallas_call(
        functools.partial(_attn_kernel, th=th, dh=dh, lam_init=lam_init, n_cast=len(casts)),
        grid=(batch, A_HEADS, nq),
        in_specs=[pl.BlockSpec(memory_space=pltpu.SMEM),
                  pl.BlockSpec((tq, dv), lambda b, h, i: (b * nq + i, h)),
                  pl.BlockSpec((seq, dv), lambda b, h, i: (b, k_blk0 + h)),
                  pl.BlockSpec((seq, dv), lambda b, h, i: (b, v_blk0 + h)),
                  vec(dv), vec(dh), vec(dh), vec(dh), vec(dh)] + c_in,
        out_specs=[pl.BlockSpec((tq, dv), lambda b, h, i: (b * nq + i, h))] + c_out,
        out_shape=[jax.ShapeDtypeStruct((t, A_HEADS * dv), BF16)] + c_shapes,
        scratch_shapes=[pltpu.VMEM((2, 2, th, LANES), F32), pltpu.VMEM((2, 2, th, LANES), F32),
                        pltpu.VMEM((2, 2, th, dv), F32), pltpu.VMEM((th, LANES), F32)],
        compiler_params=_params("arbitrary", "arbitrary", "arbitrary"),
        name="diff_attn",
    )(slopes, a_qkv, a_qkv, a_qkv, subln_g.reshape(1, dv),
      lq1.reshape(1, dh), lk1.reshape(1, dh), lq2.reshape(1, dh), lk2.reshape(1, dh), *c_ops)


def _conv_kernel(cb_ref, cc_ref, cx_ref, w_ref, y_ref):
    z = cc_ref[...] * cx_ref[...]
    rows = lax.broadcasted_iota(jnp.int32, z.shape, 0)
    w = w_ref[...]
    y = w[C_KW - 1:C_KW, :] * z
    for back in range(1, C_KW):
        zs = jnp.where(rows >= back, pltpu.roll(z, back, axis=0), 0.0)
        y = y + w[C_KW - 1 - back:C_KW - back, :] * zs
    y_ref[...] = (cb_ref[...] * y).astype(y_ref.dtype)


def _short_conv(c_bcx, conv_w, *, batch, seq, d_model, tc=256):
    cw = d_model // 4
    tc = _tile(cw, tc)
    per = cw // tc
    t = batch * seq
    return pl.pallas_call(
        _conv_kernel,
        grid=(batch, per),
        in_specs=[pl.BlockSpec((seq, tc), lambda b, j: (b, j)),
                  pl.BlockSpec((seq, tc), lambda b, j: (b, per + j)),
                  pl.BlockSpec((seq, tc), lambda b, j: (b, 2 * per + j)),
                  pl.BlockSpec((C_KW, tc), lambda b, j: (0, j))],
        out_specs=pl.BlockSpec((seq, tc), lambda b, j: (b, j)),
        out_shape=jax.ShapeDtypeStruct((t, cw), BF16),
        compiler_params=_params("parallel", "parallel"),
        name="short_conv",
    )(c_bcx, c_bcx, c_bcx, conv_w)


def kernel(x, w_in, m_b_i, m_b_f, a_lq1, a_lk1, a_lq2, a_lk2, a_subln_g, c_conv_w, w_out,
           ln1_g, ln1_b, w_up, w_down, ln2_g, ln2_b):
    batch, seq, d = x.shape
    depth = w_in.shape[0]
    t = batch * seq
    alpha = (2 * depth) ** 0.25
    chunk = _tile(seq, 256)
    nc = seq // chunk

    o_gate = 3 * d // 2
    o_aq = o_gate + N_GATES
    w_main_t = jnp.swapaxes(w_in, 1, 2).astype(BF16)
    w_tail_t = w_main_t[:, o_aq:, :]
    w_g_t = jnp.pad(w_main_t[:, o_gate:o_aq, :], ((0, 0), (0, LANES - N_GATES), (0, 0)))
    slopes = jnp.asarray([2.0 ** (-8.0 * (i + 1) / A_HEADS) for i in range(A_HEADS)], F32)

    xf = x.reshape(t, d)
    xb = xf.astype(BF16)
    for l in range(depth):
        m_qkv, gates = _matmul_nt(xb, w_main_t, l, BF16, name="in_proj_mqkv", n=d, side=w_g_t)
        m_o = _matmul_nt(xb, w_main_t, l, F32, name="in_proj_mo", n=d // 2, row0=d)
        a_qkv = _matmul_nt(xb, w_tail_t, l, BF16, name="in_proj_aqkv", n=3 * d // 4)
        c_bcx = _matmul_nt(xb, w_tail_t, l, F32, name="in_proj_conv", n=3 * d // 4, row0=3 * d // 4)
        gates = gates[:, :N_GATES].reshape(batch, nc, chunk, N_GATES).transpose(0, 1, 3, 2)
        bias = jnp.stack([m_b_i[l], m_b_f[l]])

        y_m = _mlstm(m_qkv, m_o, gates, bias, batch=batch, seq=seq, d_model=d, chunk=chunk)
        lam_init = 0.8 - 0.6 * math.exp(-0.3 * l)
        y_a, w_up_l, w_down_l, w_out_l = _diff_attn(
            a_qkv, slopes, a_subln_g[l], a_lq1[l], a_lk1[l], a_lq2[l], a_lk2[l], lam_init,
            [(w_up, l), (w_down, l), (w_out, l)], batch=batch, seq=seq, d_model=d)
        y_c = _short_conv(c_bcx, c_conv_w[l], batch=batch, seq=seq, d_model=d)
        mix = _matmul([y_m, y_a, y_c], w_out_l, F32, name="out_proj")
        xf, xb = _res_ln(xf, mix, ln1_g[l], ln1_b[l], alpha)
        hid = _matmul([xb], w_up_l, BF16, name="mlp_up", relu2=True)
        ff = _matmul_ktiled(hid, w_down_l, name="mlp_down")
        xf, xb = _res_ln(xf, ff, ln2_g[l], ln2_b[l], alpha)
    return xf.reshape(batch, seq, d)
```

```python
import functools
import math

import jax
import jax.numpy as jnp
from jax import lax
from jax.experimental import pallas as pl
from jax.experimental.pallas import tpu as pltpu

M_HEADS = 4
A_HEADS = 4
C_KW = 3
N_GATES = 2 * M_HEADS
LN_EPS = 1e-5
SUBLN_EPS = 1e-5

V7X_VMEM_LIMIT_BYTES = 56 * 1024 * 1024
LANES = 128
NEG_BIG = -1e30
LOG2E = 1.4426950408889634

F32 = jnp.float32
BF16 = jnp.bfloat16


def _tile(n, want):
    t = min(n, want)
    while n % t:
        t //= 2
    return t


def _params(*sem):
    return pltpu.CompilerParams(dimension_semantics=sem, vmem_limit_bytes=V7X_VMEM_LIMIT_BYTES)


def _lane_tile(x, reps):
    return x if reps == 1 else jnp.concatenate([x] * reps, axis=1)


def _mm_kernel(*refs, n_x, relu2, res_scale):
    x_refs, w_ref, o_ref = refs[:n_x], refs[n_x], refs[-1]
    acc = None
    k0 = 0
    for x_ref in x_refs:
        kp = x_ref.shape[1]
        part = jnp.dot(x_ref[...], w_ref[k0:k0 + kp, :], preferred_element_type=F32)
        acc = part if acc is None else acc + part
        k0 += kp
    if relu2:
        acc = jnp.square(jnp.maximum(acc, 0.0))
    if res_scale is not None:
        acc = res_scale * refs[n_x + 1][...] + acc
    o_ref[...] = acc.astype(o_ref.dtype)


def _matmul(xs, w, out_dtype, *, name, relu2=False, residual=None, res_scale=None, tm=1024, tn=1024):
    m = xs[0].shape[0]
    k, n = w.shape
    tm, tn = _tile(m, tm), _tile(n, tn)
    assert sum(xp.shape[1] for xp in xs) == k
    tile = pl.BlockSpec((tm, tn), lambda i, j: (i, j))
    extra = [] if residual is None else [residual]
    return pl.pallas_call(
        functools.partial(_mm_kernel, n_x=len(xs), relu2=relu2, res_scale=res_scale),
        grid=(m // tm, n // tn),
        in_specs=[pl.BlockSpec((tm, xp.shape[1]), lambda i, j: (i, 0)) for xp in xs]
        + [pl.BlockSpec((k, tn), lambda i, j: (0, j))] + [tile] * len(extra),
        out_specs=tile,
        out_shape=jax.ShapeDtypeStruct((m, n), out_dtype),
        compiler_params=_params("parallel", "parallel"),
        name=name,
    )(*xs, w, *extra)


_NT_DIMS = (((1,), (1,)), ((), ()))


def _mm_nt_kernel(x_ref, wt_ref, *rest):
    if len(rest) == 1:
        (o_ref,) = rest
    else:
        ws_ref, o_ref, os_ref = rest

        @pl.when(pl.program_id(1) == 0)
        def _():
            os_ref[...] = lax.dot_general(x_ref[...], ws_ref[...], _NT_DIMS, preferred_element_type=F32)

    o_ref[...] = lax.dot_general(x_ref[...], wt_ref[...], _NT_DIMS, preferred_element_type=F32).astype(o_ref.dtype)


def _matmul_nt(x, wt, layer, out_dtype, *, name, n, row0=0, side=None, tm=1024, tn=1024):
    m, k = x.shape
    tm, tn = _tile(m, tm), _tile(n, tn)
    assert row0 % tn == 0 and wt.shape[2] == k
    rb0 = row0 // tn
    operands = [x, wt]
    in_specs = [pl.BlockSpec((tm, k), lambda i, j: (i, 0)),
                pl.BlockSpec((None, tn, k), lambda i, j: (layer, rb0 + j, 0))]
    out_specs = pl.BlockSpec((tm, tn), lambda i, j: (i, j))
    out_shape = jax.ShapeDtypeStruct((m, n), out_dtype)
    if side is not None:
        r = side.shape[1]
        operands.append(side)
        in_specs.append(pl.BlockSpec((None, r, k), lambda i, j: (layer, 0, 0)))
        out_specs = [out_specs, pl.BlockSpec((tm, r), lambda i, j: (i, 0))]
        out_shape = [out_shape, jax.ShapeDtypeStruct((m, r), F32)]
    return pl.pallas_call(
        _mm_nt_kernel,
        grid=(m // tm, n // tn),
        in_specs=in_specs,
        out_specs=out_specs,
        out_shape=out_shape,
        compiler_params=_params("parallel", "arbitrary"),
        name=name,
    )(*operands)


def _mm_acc_kernel(x_ref, w_ref, res_ref, o_ref, *, res_scale):
    @pl.when(pl.program_id(2) == 0)
    def _():
        o_ref[...] = res_scale * res_ref[...]

    o_ref[...] += jnp.dot(x_ref[...], w_ref[...], preferred_element_type=F32)


def _matmul_ktiled(x, w, residual, res_scale, *, name, tm=1024, tn=1024, tk=2048):
    m, k = x.shape
    n = w.shape[1]
    tm, tn, tk = _tile(m, tm), _tile(n, tn), _tile(k, tk)
    tile = pl.BlockSpec((tm, tn), lambda i, j, kk: (i, j))
    return pl.pallas_call(
        functools.partial(_mm_acc_kernel, res_scale=res_scale),
        grid=(m // tm, n // tn, k // tk),
        in_specs=[pl.BlockSpec((tm, tk), lambda i, j, kk: (i, kk)),
                  pl.BlockSpec((tk, tn), lambda i, j, kk: (kk, j)), tile],
        out_specs=tile,
        out_shape=jax.ShapeDtypeStruct((m, n), F32),
        compiler_params=_params("parallel", "parallel", "arbitrary"),
        name=name,
    )(x, w, residual)


def _ln_kernel(r_ref, g_ref, b_ref, o_ref, ob_ref):
    r = r_ref[...]
    mu = jnp.mean(r, axis=-1, keepdims=True)
    d = r - mu
    var = jnp.mean(d * d, axis=-1, keepdims=True)
    o = d * lax.rsqrt(var + LN_EPS) * g_ref[...] + b_ref[...]
    o_ref[...] = o
    ob_ref[...] = o.astype(BF16)


def _layer_norm(r, g, b, *, tr=256):
    t, d = r.shape
    tr = _tile(t, tr)
    row = pl.BlockSpec((tr, d), lambda i: (i, 0))
    vec = pl.BlockSpec((1, d), lambda i: (0, 0))
    return pl.pallas_call(
        _ln_kernel,
        grid=(t // tr,),
        in_specs=[row, vec, vec],
        out_specs=[row, row],
        out_shape=[jax.ShapeDtypeStruct((t, d), F32), jax.ShapeDtypeStruct((t, d), BF16)],
        compiler_params=_params("parallel"),
        name="layer_norm",
    )(r, g.reshape(1, d), b.reshape(1, d))


def _cast_side_stream(casts, n_steps, step_of):
    operands, in_specs, out_specs, out_shapes = [], [], [], []
    for w, layer in casts:
        _, r, c = w.shape
        rows = r // n_steps
        assert rows * n_steps == r and rows % 16 == 0
        operands.append(w)
        in_specs.append(pl.BlockSpec((None, rows, c), lambda *g, layer=layer: (layer, step_of(*g), 0)))
        out_specs.append(pl.BlockSpec((rows, c), lambda *g: (step_of(*g), 0)))
        out_shapes.append(jax.ShapeDtypeStruct((r, c), BF16))
    return operands, in_specs, out_specs, out_shapes


def _round_weights(src_refs, dst_refs):
    for src, dst in zip(src_refs, dst_refs, strict=True):
        dst[...] = src[...].astype(dst.dtype)


def _mlstm_kernel(bias_ref, q_ref, k_ref, v_ref, og_ref, gate_ref, y_ref, c_ref, n_ref, m_ref, *, dqk, dv):
    @pl.when(pl.program_id(1) == 0)
    def _():
        c_ref[...] = jnp.zeros_like(c_ref)
        n_ref[...] = jnp.zeros_like(n_ref)
        m_ref[...] = jnp.zeros_like(m_ref)

    ln = q_ref.shape[0]
    scale = float(dqk) ** -0.5
    rows = lax.broadcasted_iota(jnp.int32, (ln, ln), 0)
    cols = lax.broadcasted_iota(jnp.int32, (ln, ln), 1)
    causal = cols <= rows
    eye = cols == rows

    def to_col(row):
        return jnp.sum(jnp.where(eye, jnp.broadcast_to(row, (ln, ln)), 0.0), axis=1, keepdims=True)

    def to_row(col):
        return jnp.sum(jnp.where(eye, jnp.broadcast_to(col, (ln, ln)), 0.0), axis=0, keepdims=True)

    for head in range(M_HEADS):
        q = q_ref[:, head * dqk:(head + 1) * dqk]
        k = k_ref[:, head * dqk:(head + 1) * dqk]
        v = v_ref[:, head * dv:(head + 1) * dv]
        i_row = gate_ref[head:head + 1, :] + bias_ref[0, head]
        f_row = gate_ref[M_HEADS + head:M_HEADS + head + 1, :] + bias_ref[1, head]
        lf_row = jnp.minimum(f_row, 0.0) - jnp.log(1.0 + jnp.exp(-jnp.abs(f_row)))

        b_col = jnp.sum(jnp.where(causal, jnp.broadcast_to(lf_row, (ln, ln)), 0.0), axis=1, keepdims=True)
        b_row = to_row(b_col)

        m_prev = m_ref[head]
        dmat = jnp.where(causal, b_col - b_row + i_row, -jnp.inf)
        inter = b_col + m_prev
        m_row = jnp.maximum(inter, jnp.max(dmat, axis=1, keepdims=True))
        qk = lax.dot_general(q, k, _NT_DIMS, preferred_element_type=F32) * scale
        w = qk * jnp.exp(dmat - m_row)
        g = jnp.exp(inter - m_row)

        c_old = c_ref[head]
        n_old = n_ref[head]
        q_c = jnp.dot(q, c_old.astype(BF16), preferred_element_type=F32)
        num = jnp.dot(w.astype(BF16), v, preferred_element_type=F32) + g * q_c
        den = (jnp.sum(w, axis=1, keepdims=True)
               + g * jnp.sum(q.astype(F32) * n_old, axis=1, keepdims=True))
        hid = num * (1.0 / jnp.maximum(jnp.abs(den), jnp.exp(-m_row)))
        gate = 1.0 / (1.0 + jnp.exp(-og_ref[:, head * dv:(head + 1) * dv]))
        y_ref[:, head * dv:(head + 1) * dv] = (gate * hid).astype(y_ref.dtype)

        m_new = m_row[ln - 1:ln, :]
        b_last = b_col[ln - 1:ln, :]
        wl_row = jnp.exp(b_last - b_row + i_row - m_new)
        gl = jnp.exp(b_last + m_prev - m_new)
        wl_col = to_col(wl_row) * scale
        v_w = (wl_col * v.astype(F32)).astype(BF16)
        kv = lax.dot_general(k, v_w, (((0,), (0,)), ((), ())), preferred_element_type=F32)
        c_ref[head] = gl * c_old + kv
        n_ref[head] = gl * n_old + jnp.sum(wl_col * k.astype(F32), axis=0, keepdims=True)
        m_ref[head] = m_new


def _mlstm(m_qkv, m_o, gates, bias, *, batch, seq, d_model, chunk):
    dqk, dv = d_model // 16, d_model // 8
    qw, vw = M_HEADS * dqk, M_HEADS * dv
    nc = seq // chunk
    t = batch * seq
    return pl.pallas_call(
        functools.partial(_mlstm_kernel, dqk=dqk, dv=dv),
        grid=(batch, nc),
        in_specs=[pl.BlockSpec(memory_space=pltpu.SMEM),
                  pl.BlockSpec((chunk, qw), lambda b, c: (b * nc + c, 0)),
                  pl.BlockSpec((chunk, qw), lambda b, c: (b * nc + c, 1)),
                  pl.BlockSpec((chunk, vw), lambda b, c: (b * nc + c, 1)),
                  pl.BlockSpec((chunk, vw), lambda b, c: (b * nc + c, 0)),
                  pl.BlockSpec((None, None, N_GATES, chunk), lambda b, c: (b, c, 0, 0))],
        out_specs=pl.BlockSpec((chunk, vw), lambda b, c: (b * nc + c, 0)),
        out_shape=jax.ShapeDtypeStruct((t, vw), BF16),
        scratch_shapes=[pltpu.VMEM((M_HEADS, dqk, dv), F32), pltpu.VMEM((M_HEADS, 1, dqk), F32),
                        pltpu.VMEM((M_HEADS, 1, 1), F32)],
        compiler_params=_params("parallel", "arbitrary"),
        name="mlstm",
    )(bias, m_qkv, m_qkv, m_qkv, m_o, gates)


def _attn_kernel(*refs, th, dh, lam_init, n_cast):
    slope_ref, q_ref, k_ref, v_ref, g_ref, lq1_ref, lk1_ref, lq2_ref, lk2_ref = refs[:9]
    y_ref = refs[9 + n_cast]
    m_ref, l_ref, acc_ref, rowt_ref = refs[10 + 2 * n_cast:]
    _round_weights(refs[9:9 + n_cast], refs[10 + n_cast:10 + 2 * n_cast])

    head = pl.program_id(1)
    qi = pl.program_id(2)
    dv = acc_ref.shape[-1]
    slope2 = slope_ref[head] * LOG2E
    cs = (float(dh) ** -0.5) * LOG2E

    m_ref[...] = jnp.full_like(m_ref, NEG_BIG)
    l_ref[...] = jnp.zeros_like(l_ref)
    acc_ref[...] = jnp.zeros_like(acc_ref)
    col_bias = slope2 * lax.broadcasted_iota(jnp.int32, (1, th), 1).astype(F32)
    rowt_ref[...] = slope2 * lax.broadcasted_iota(jnp.int32, (th, LANES), 0).astype(F32)

    def half_step(hf, k_blk, v_blk, causal, c0):
        q = q_ref[hf * th:(hf + 1) * th, :]
        row_off = c0 - rowt_ref[...]
        for mp in range(2):
            t2 = lax.dot_general(q[:, mp * dh:(mp + 1) * dh], k_blk[:, mp * dh:(mp + 1) * dh],
                                 _NT_DIMS, preferred_element_type=F32) * cs + col_bias
            if causal is not None:
                t2 = jnp.where(causal, t2, NEG_BIG)
            m_old = m_ref[hf, mp]
            m_new = jnp.maximum(m_old, jnp.max(t2, axis=1, keepdims=True) + row_off)
            p = jnp.exp2(t2 - _lane_tile(m_new - row_off, th // LANES))
            alpha = jnp.exp2(m_old - m_new)
            p_lanes = p[:, :LANES]
            for c in range(1, th // LANES):
                p_lanes = p_lanes + p[:, c * LANES:(c + 1) * LANES]
            l_ref[hf, mp] = alpha * l_ref[hf, mp] + p_lanes
            acc_ref[hf, mp] = (_lane_tile(alpha, dv // LANES) * acc_ref[hf, mp]
                               + jnp.dot(p.astype(BF16), v_blk, preferred_element_type=F32))
            m_ref[hf, mp] = m_new

    def kv_block(kb):
        ks = pl.multiple_of(kb * th, th)
        return k_ref[pl.ds(ks, th), :], v_ref[pl.ds(ks, th), :]

    def offset(n_blocks):
        return -slope2 * (n_blocks * th).astype(F32)

    def body(kb, carry):
        k_blk, v_blk = kv_block(kb)
        half_step(0, k_blk, v_blk, None, offset(2 * qi - kb))
        half_step(1, k_blk, v_blk, None, offset(2 * qi + 1 - kb))
        return carry

    lax.fori_loop(0, 2 * qi, body, 0)
    causal = (lax.broadcasted_iota(jnp.int32, (th, th), 0) >= lax.broadcasted_iota(jnp.int32, (th, th), 1))
    k_blk, v_blk = kv_block(2 * qi)
    half_step(0, k_blk, v_blk, causal, 0.0)
    half_step(1, k_blk, v_blk, None, -slope2 * float(th))
    k_blk, v_blk = kv_block(2 * qi + 1)
    half_step(1, k_blk, v_blk, causal, 0.0)

    lam = (jnp.exp(jnp.sum(lq1_ref[...] * lk1_ref[...], axis=1, keepdims=True))
           - jnp.exp(jnp.sum(lq2_ref[...] * lk2_ref[...], axis=1, keepdims=True)) + lam_init)
    for hf in range(2):
        inv = [1.0 / jnp.sum(l_ref[hf, mp], axis=1, keepdims=True) for mp in range(2)]
        o = acc_ref[hf, 0] * inv[0] - lam * (acc_ref[hf, 1] * inv[1])
        o = o * lax.rsqrt(jnp.mean(o * o, axis=-1, keepdims=True) + SUBLN_EPS) * g_ref[...]
        y_ref[hf * th:(hf + 1) * th, :] = (o * (1.0 - lam_init)).astype(y_ref.dtype)


def _diff_attn(a_qkv, slopes, subln_g, lq1, lk1, lq2, lk2, lam_init, casts, *, batch, seq, d_model, th=512):
    dh = d_model // 32
    dv = 2 * dh
    th = _tile(seq // 2, th)
    tq = 2 * th
    nq = seq // tq
    t = batch * seq
    k_blk0 = A_HEADS
    v_blk0 = 2 * A_HEADS
    vec = lambda n: pl.BlockSpec((1, n), lambda b, h, i: (0, 0))
    c_ops, c_in, c_out, c_shapes = _cast_side_stream(casts, batch * A_HEADS * nq,
                                                     lambda b, h, i: (b * A_HEADS + h) * nq + i)
    return pl.pallas_call(
        functools.partial(_attn_kernel, th=th, dh=dh, lam_init=lam_init, n_cast=len(casts)),
        grid=(batch, A_HEADS, nq),
        in_specs=[pl.BlockSpec(memory_space=pltpu.SMEM),
                  pl.BlockSpec((tq, dv), lambda b, h, i: (b * nq + i, h)),
                  pl.BlockSpec((seq, dv), lambda b, h, i: (b, k_blk0 + h)),
                  pl.BlockSpec((seq, dv), lambda b, h, i: (b, v_blk0 + h)),
                  vec(dv), vec(dh), vec(dh), vec(dh), vec(dh)] + c_in,
        out_specs=[pl.BlockSpec((tq, dv), lambda b, h, i: (b * nq + i, h))] + c_out,
        out_shape=[jax.ShapeDtypeStruct((t, A_HEADS * dv), BF16)] + c_shapes,
        scratch_shapes=[pltpu.VMEM((2, 2, th, LANES), F32), pltpu.VMEM((2, 2, th, LANES), F32),
                        pltpu.VMEM((2, 2, th, dv), F32), pltpu.VMEM((th, LANES), F32)],
        compiler_params=_params("arbitrary", "arbitrary", "arbitrary"),
        name="diff_attn",
    )(slopes, a_qkv, a_qkv, a_qkv, subln_g.reshape(1, dv),
      lq1.reshape(1, dh), lk1.reshape(1, dh), lq2.reshape(1, dh), lk2.reshape(1, dh), *c_ops)


def _conv_kernel(cb_ref, cc_ref, cx_ref, w_ref, y_ref):
    z = cc_ref[...] * cx_ref[...]
    rows = lax.broadcasted_iota(jnp.int32, z.shape, 0)
    w = w_ref[...]
    y = w[C_KW - 1:C_KW, :] * z
    for back in range(1, C_KW):
        zs = jnp.where(rows >= back, pltpu.roll(z, back, axis=0), 0.0)
        y = y + w[C_KW - 1 - back:C_KW - back, :] * zs
    y_ref[...] = (cb_ref[...] * y).astype(y_ref.dtype)


def _short_conv(c_bcx, conv_w, *, batch, seq, d_model, tc=256):
    cw = d_model // 4
    tc = _tile(cw, tc)
    per = cw // tc
    t = batch * seq
    return pl.pallas_call(
        _conv_kernel,
        grid=(batch, per),
        in_specs=[pl.BlockSpec((seq, tc), lambda b, j: (b, j)),
                  pl.BlockSpec((seq, tc), lambda b, j: (b, per + j)),
                  pl.BlockSpec((seq, tc), lambda b, j: (b, 2 * per + j)),
                  pl.BlockSpec((C_KW, tc), lambda b, j: (0, j))],
        out_specs=pl.BlockSpec((seq, tc), lambda b, j: (b, j)),
        out_shape=jax.ShapeDtypeStruct((t, cw), BF16),
        compiler_params=_params("parallel", "parallel"),
        name="short_conv",
    )(c_bcx, c_bcx, c_bcx, conv_w)


def kernel(x, w_in, m_b_i, m_b_f, a_lq1, a_lk1, a_lq2, a_lk2, a_subln_g, c_conv_w, w_out,
           ln1_g, ln1_b, w_up, w_down, ln2_g, ln2_b):
    batch, seq, d = x.shape
    depth = w_in.shape[0]
    t = batch * seq
    alpha = (2 * depth) ** 0.25
    chunk = _tile(seq, 256)
    nc = seq // chunk

    o_gate = 3 * d // 2
    o_aq = o_gate + N_GATES
    w_main_t = jnp.swapaxes(w_in, 1, 2).astype(BF16)
    w_tail_t = w_main_t[:, o_aq:, :]
    w_g_t = jnp.pad(w_main_t[:, o_gate:o_aq, :], ((0, 0), (0, LANES - N_GATES), (0, 0)))
    slopes = jnp.asarray([2.0 ** (-8.0 * (i + 1) / A_HEADS) for i in range(A_HEADS)], F32)

    xf = x.reshape(t, d)
    xb = xf.astype(BF16)
    for l in range(depth):
        m_qkv, gates = _matmul_nt(xb, w_main_t, l, BF16, name="in_proj_mqkv", n=d, side=w_g_t)
        m_o = _matmul_nt(xb, w_main_t, l, F32, name="in_proj_mo", n=d // 2, row0=d)
        a_qkv = _matmul_nt(xb, w_tail_t, l, BF16, name="in_proj_aqkv", n=3 * d // 4)
        c_bcx = _matmul_nt(xb, w_tail_t, l, F32, name="in_proj_conv", n=3 * d // 4, row0=3 * d // 4)
        gates = gates[:, :N_GATES].reshape(batch, nc, chunk, N_GATES).transpose(0, 1, 3, 2)
        bias = jnp.stack([m_b_i[l], m_b_f[l]])

        y_m = _mlstm(m_qkv, m_o, gates, bias, batch=batch, seq=seq, d_model=d, chunk=chunk)
        lam_init = 0.8 - 0.6 * math.exp(-0.3 * l)
        y_a, w_up_l, w_down_l, w_out_l = _diff_attn(
            a_qkv, slopes, a_subln_g[l], a_lq1[l], a_lk1[l], a_lq2[l], a_lk2[l], lam_init,
            [(w_up, l), (w_down, l), (w_out, l)], batch=batch, seq=seq, d_model=d)
        y_c = _short_conv(c_bcx, c_conv_w[l], batch=batch, seq=seq, d_model=d)
        r1 = _matmul([y_m, y_a, y_c], w_out_l, F32, name="out_proj", residual=xf, res_scale=alpha)
        xf, xb = _layer_norm(r1, ln1_g[l], ln1_b[l])
        hid = _matmul([xb], w_up_l, BF16, name="mlp_up", relu2=True)
        r2 = _matmul_ktiled(hid, w_down_l, xf, alpha, name="mlp_down")
        xf, xb = _layer_norm(r2, ln2_g[l], ln2_b[l])
    return xf.reshape(batch, seq, d)
```

```python
import functools
import math

import jax
import jax.numpy as jnp
from jax import lax
from jax.experimental import pallas as pl
from jax.experimental.pallas import tpu as pltpu

M_HEADS = 4
A_HEADS = 4
C_KW = 3
N_GATES = 2 * M_HEADS
LN_EPS = 1e-5
SUBLN_EPS = 1e-5

V7X_VMEM_LIMIT_BYTES = 60 * 1024 * 1024
LANES = 128
NEG_BIG = -1e30
LOG2E = 1.4426950408889634

F32 = jnp.float32
BF16 = jnp.bfloat16


def _tile(n, want):
    t = min(n, want)
    while n % t:
        t //= 2
    return t


def _params(*sem):
    return pltpu.CompilerParams(dimension_semantics=sem, vmem_limit_bytes=V7X_VMEM_LIMIT_BYTES)


def _lane_tile(x, reps):
    return x if reps == 1 else jnp.concatenate([x] * reps, axis=1)


def _residual_tile(res_refs):
    if len(res_refs) == 1:
        return res_refs[0][...]
    r_ref, mu_ref, rstd_ref, g_ref, b_ref = res_refs
    return (r_ref[...] - mu_ref[...]) * rstd_ref[...] * g_ref[...] + b_ref[...]


def _residual_specs(residual, tm, tn):
    tile = pl.BlockSpec((tm, tn), lambda i, j, *_: (i, j))
    if len(residual) == 1:
        return [tile]
    col = pl.BlockSpec((tm, 1), lambda i, j, *_: (i, 0))
    row = pl.BlockSpec((1, tn), lambda i, j, *_: (0, j))
    return [tile, col, col, row, row]


def _mm_kernel(*refs, n_x, relu2, res_scale):
    x_refs, w_ref, o_ref = refs[:n_x], refs[n_x], refs[-1]
    acc = None
    k0 = 0
    for x_ref in x_refs:
        kp = x_ref.shape[1]
        part = jnp.dot(x_ref[...], w_ref[k0:k0 + kp, :], preferred_element_type=F32)
        acc = part if acc is None else acc + part
        k0 += kp
    if relu2:
        acc = jnp.square(jnp.maximum(acc, 0.0))
    if res_scale is not None:
        acc = res_scale * _residual_tile(refs[n_x + 1:-1]) + acc
    o_ref[...] = acc.astype(o_ref.dtype)


def _matmul(xs, w, out_dtype, *, name, relu2=False, residual=(), res_scale=None, tm=1024, tn=1024):
    m = xs[0].shape[0]
    k, n = w.shape
    tm, tn = _tile(m, tm), _tile(n, tn)
    assert sum(xp.shape[1] for xp in xs) == k
    return pl.pallas_call(
        functools.partial(_mm_kernel, n_x=len(xs), relu2=relu2, res_scale=res_scale),
        grid=(m // tm, n // tn),
        in_specs=[pl.BlockSpec((tm, xp.shape[1]), lambda i, j: (i, 0)) for xp in xs]
        + [pl.BlockSpec((k, tn), lambda i, j: (0, j))] + (_residual_specs(residual, tm, tn) if residual else []),
        out_specs=pl.BlockSpec((tm, tn), lambda i, j: (i, j)),
        out_shape=jax.ShapeDtypeStruct((m, n), out_dtype),
        compiler_params=_params("parallel", "parallel"),
        name=name,
    )(*xs, w, *residual)


_NT_DIMS = (((1,), (1,)), ((), ()))


def _mm_nt_kernel(x_ref, wt_ref, *rest):
    if len(rest) == 1:
        (o_ref,) = rest
    else:
        ws_ref, o_ref, os_ref = rest

        @pl.when(pl.program_id(1) == 0)
        def _():
            os_ref[...] = lax.dot_general(x_ref[...], ws_ref[...], _NT_DIMS, preferred_element_type=F32)

    o_ref[...] = lax.dot_general(x_ref[...], wt_ref[...], _NT_DIMS, preferred_element_type=F32).astype(o_ref.dtype)


def _matmul_nt(x, wt, layer, out_dtype, *, name, n, row0=0, side=None, tm=1024, tn=1024):
    m, k = x.shape
    tm, tn = _tile(m, tm), _tile(n, tn)
    assert row0 % tn == 0 and wt.shape[2] == k
    rb0 = row0 // tn
    operands = [x, wt]
    in_specs = [pl.BlockSpec((tm, k), lambda i, j: (i, 0)),
                pl.BlockSpec((None, tn, k), lambda i, j: (layer, rb0 + j, 0))]
    out_specs = pl.BlockSpec((tm, tn), lambda i, j: (i, j))
    out_shape = jax.ShapeDtypeStruct((m, n), out_dtype)
    if side is not None:
        r = side.shape[1]
        operands.append(side)
        in_specs.append(pl.BlockSpec((None, r, k), lambda i, j: (layer, 0, 0)))
        out_specs = [out_specs, pl.BlockSpec((tm, r), lambda i, j: (i, 0))]
        out_shape = [out_shape, jax.ShapeDtypeStruct((m, r), F32)]
    return pl.pallas_call(
        _mm_nt_kernel,
        grid=(m // tm, n // tn),
        in_specs=in_specs,
        out_specs=out_specs,
        out_shape=out_shape,
        compiler_params=_params("parallel", "arbitrary"),
        name=name,
    )(*operands)


def _mm_acc_kernel(x_ref, w_ref, *rest, res_scale):
    o_ref = rest[-1]

    @pl.when(pl.program_id(2) == 0)
    def _():
        o_ref[...] = res_scale * _residual_tile(rest[:-1])

    o_ref[...] += jnp.dot(x_ref[...], w_ref[...], preferred_element_type=F32)


def _matmul_ktiled(x, w, residual, res_scale, *, name, tm=1024, tn=1024, tk=2048):
    m, k = x.shape
    n = w.shape[1]
    tm, tn, tk = _tile(m, tm), _tile(n, tn), _tile(k, tk)
    return pl.pallas_call(
        functools.partial(_mm_acc_kernel, res_scale=res_scale),
        grid=(m // tm, n // tn, k // tk),
        in_specs=[pl.BlockSpec((tm, tk), lambda i, j, kk: (i, kk)),
                  pl.BlockSpec((tk, tn), lambda i, j, kk: (kk, j))] + _residual_specs(residual, tm, tn),
        out_specs=pl.BlockSpec((tm, tn), lambda i, j, kk: (i, j)),
        out_shape=jax.ShapeDtypeStruct((m, n), F32),
        compiler_params=_params("parallel", "parallel", "arbitrary"),
        name=name,
    )(x, w, *residual)


def _ln_kernel(r_ref, g_ref, b_ref, *out_refs, keep_f32):
    r = r_ref[...]
    mu = jnp.mean(r, axis=-1, keepdims=True)
    d = r - mu
    rstd = lax.rsqrt(jnp.mean(d * d, axis=-1, keepdims=True) + LN_EPS)
    o = d * rstd * g_ref[...] + b_ref[...]
    if keep_f32:
        out_refs[0][...] = o
    else:
        ob_ref, mu_ref, rstd_ref = out_refs
        ob_ref[...] = o.astype(BF16)
        mu_ref[...] = mu
        rstd_ref[...] = rstd


def _layer_norm(r, g, b, *, keep_f32, tr=256):
    t, d = r.shape
    tr = _tile(t, tr)
    row = pl.BlockSpec((tr, d), lambda i: (i, 0))
    vec = pl.BlockSpec((1, d), lambda i: (0, 0))
    stat = pl.BlockSpec((tr, 1), lambda i: (i, 0))
    stat_shape = jax.ShapeDtypeStruct((t, 1), F32)
    return pl.pallas_call(
        functools.partial(_ln_kernel, keep_f32=keep_f32),
        grid=(t // tr,),
        in_specs=[row, vec, vec],
        out_specs=[row] if keep_f32 else [row, stat, stat],
        out_shape=([jax.ShapeDtypeStruct((t, d), F32)] if keep_f32
                   else [jax.ShapeDtypeStruct((t, d), BF16), stat_shape, stat_shape]),
        compiler_params=_params("parallel"),
        name="layer_norm",
    )(r, g.reshape(1, d), b.reshape(1, d))


def _cast_side_stream(casts, n_steps, step_of):
    operands, in_specs, out_specs, out_shapes = [], [], [], []
    for w, layer in casts:
        _, r, c = w.shape
        rows = r // n_steps
        assert rows * n_steps == r and rows % 16 == 0
        operands.append(w)
        in_specs.append(pl.BlockSpec((None, rows, c), lambda *g, layer=layer: (layer, step_of(*g), 0)))
        out_specs.append(pl.BlockSpec((rows, c), lambda *g: (step_of(*g), 0)))
        out_shapes.append(jax.ShapeDtypeStruct((r, c), BF16))
    return operands, in_specs, out_specs, out_shapes


def _round_weights(src_refs, dst_refs):
    for src, dst in zip(src_refs, dst_refs, strict=True):
        dst[...] = src[...].astype(dst.dtype)


def _mlstm_kernel(bias_ref, q_ref, k_ref, v_ref, og_ref, gate_ref, y_ref, c_ref, n_ref, m_ref, *, dqk, dv):
    @pl.when(pl.program_id(1) == 0)
    def _():
        c_ref[...] = jnp.zeros_like(c_ref)
        n_ref[...] = jnp.zeros_like(n_ref)
        m_ref[...] = jnp.zeros_like(m_ref)

    ln = q_ref.shape[0]
    scale = float(dqk) ** -0.5
    rows = lax.broadcasted_iota(jnp.int32, (ln, ln), 0)
    cols = lax.broadcasted_iota(jnp.int32, (ln, ln), 1)
    causal = cols <= rows
    eye = cols == rows

    def to_col(row):
        return jnp.sum(jnp.where(eye, jnp.broadcast_to(row, (ln, ln)), 0.0), axis=1, keepdims=True)

    def to_row(col):
        return jnp.sum(jnp.where(eye, jnp.broadcast_to(col, (ln, ln)), 0.0), axis=0, keepdims=True)

    for head in range(M_HEADS):
        q = q_ref[:, head * dqk:(head + 1) * dqk]
        k = k_ref[:, head * dqk:(head + 1) * dqk]
        v = v_ref[:, head * dv:(head + 1) * dv]
        i_row = gate_ref[head:head + 1, :] + bias_ref[0, head]
        f_row = gate_ref[M_HEADS + head:M_HEADS + head + 1, :] + bias_ref[1, head]
        lf_row = jnp.minimum(f_row, 0.0) - jnp.log(1.0 + jnp.exp(-jnp.abs(f_row)))

        b_col = jnp.sum(jnp.where(causal, jnp.broadcast_to(lf_row, (ln, ln)), 0.0), axis=1, keepdims=True)
        b_row = to_row(b_col)

        m_prev = m_ref[head]
        dmat = jnp.where(causal, b_col - b_row + i_row, -jnp.inf)
        inter = b_col + m_prev
        m_row = jnp.maximum(inter, jnp.max(dmat, axis=1, keepdims=True))
        qk = lax.dot_general(q, k, _NT_DIMS, preferred_element_type=F32) * scale
        w = qk * jnp.exp(dmat - m_row)
        g = jnp.exp(inter - m_row)

        c_old = c_ref[head]
        n_old = n_ref[head]
        q_c = jnp.dot(q, c_old.astype(BF16), preferred_element_type=F32)
        num = jnp.dot(w.astype(BF16), v, preferred_element_type=F32) + g * q_c
        den = (jnp.sum(w, axis=1, keepdims=True)
               + g * jnp.sum(q.astype(F32) * n_old, axis=1, keepdims=True))
        hid = num * (1.0 / jnp.maximum(jnp.abs(den), jnp.exp(-m_row)))
        gate = 1.0 / (1.0 + jnp.exp(-og_ref[:, head * dv:(head + 1) * dv]))
        y_ref[:, head * dv:(head + 1) * dv] = (gate * hid).astype(y_ref.dtype)

        m_new = m_row[ln - 1:ln, :]
        b_last = b_col[ln - 1:ln, :]
        wl_row = jnp.exp(b_last - b_row + i_row - m_new)
        gl = jnp.exp(b_last + m_prev - m_new)
        wl_col = to_col(wl_row) * scale
        v_w = (wl_col * v.astype(F32)).astype(BF16)
        kv = lax.dot_general(k, v_w, (((0,), (0,)), ((), ())), preferred_element_type=F32)
        c_ref[head] = gl * c_old + kv
        n_ref[head] = gl * n_old + jnp.sum(wl_col * k.astype(F32), axis=0, keepdims=True)
        m_ref[head] = m_new


def _mlstm(m_qkv, m_o, gates, bias, *, batch, seq, d_model, chunk):
    dqk, dv = d_model // 16, d_model // 8
    qw, vw = M_HEADS * dqk, M_HEADS * dv
    nc = seq // chunk
    t = batch * seq
    return pl.pallas_call(
        functools.partial(_mlstm_kernel, dqk=dqk, dv=dv),
        grid=(batch, nc),
        in_specs=[pl.BlockSpec(memory_space=pltpu.SMEM),
                  pl.BlockSpec((chunk, qw), lambda b, c: (b * nc + c, 0)),
                  pl.BlockSpec((chunk, qw), lambda b, c: (b * nc + c, 1)),
                  pl.BlockSpec((chunk, vw), lambda b, c: (b * nc + c, 1)),
                  pl.BlockSpec((chunk, vw), lambda b, c: (b * nc + c, 0)),
                  pl.BlockSpec((None, None, N_GATES, chunk), lambda b, c: (b, c, 0, 0))],
        out_specs=pl.BlockSpec((chunk, vw), lambda b, c: (b * nc + c, 0)),
        out_shape=jax.ShapeDtypeStruct((t, vw), BF16),
        scratch_shapes=[pltpu.VMEM((M_HEADS, dqk, dv), F32), pltpu.VMEM((M_HEADS, 1, dqk), F32),
                        pltpu.VMEM((M_HEADS, 1, 1), F32)],
        compiler_params=_params("parallel", "arbitrary"),
        name="mlstm",
    )(bias, m_qkv, m_qkv, m_qkv, m_o, gates)


def _attn_kernel(*refs, th, dh, lam_init, n_cast):
    slope_ref, q_ref, k_ref, v_ref, g_ref, lq1_ref, lk1_ref, lq2_ref, lk2_ref = refs[:9]
    y_ref = refs[9 + n_cast]
    m_ref, l_ref, acc_ref, rowt_ref = refs[10 + 2 * n_cast:]
    _round_weights(refs[9:9 + n_cast], refs[10 + n_cast:10 + 2 * n_cast])

    head = pl.program_id(1)
    qi = pl.program_id(2)
    dv = acc_ref.shape[-1]
    slope2 = slope_ref[head] * LOG2E
    cs = (float(dh) ** -0.5) * LOG2E

    m_ref[...] = jnp.full_like(m_ref, NEG_BIG)
    l_ref[...] = jnp.zeros_like(l_ref)
    acc_ref[...] = jnp.zeros_like(acc_ref)
    col_bias = slope2 * lax.broadcasted_iota(jnp.int32, (1, th), 1).astype(F32)
    rowt_ref[...] = slope2 * lax.broadcasted_iota(jnp.int32, (th, LANES), 0).astype(F32)

    def half_step(hf, k_blk, v_blk, causal, c0):
        q = q_ref[hf * th:(hf + 1) * th, :]
        row_off = c0 - rowt_ref[...]
        for mp in range(2):
            t2 = lax.dot_general(q[:, mp * dh:(mp + 1) * dh], k_blk[:, mp * dh:(mp + 1) * dh],
                                 _NT_DIMS, preferred_element_type=F32) * cs + col_bias
            if causal is not None:
                t2 = jnp.where(causal, t2, NEG_BIG)
            m_old = m_ref[hf, mp]
            m_new = jnp.maximum(m_old, jnp.max(t2, axis=1, keepdims=True) + row_off)
            p = jnp.exp2(t2 - _lane_tile(m_new - row_off, th // LANES))
            alpha = jnp.exp2(m_old - m_new)
            p_lanes = p[:, :LANES]
            for c in range(1, th // LANES):
                p_lanes = p_lanes + p[:, c * LANES:(c + 1) * LANES]
            l_ref[hf, mp] = alpha * l_ref[hf, mp] + p_lanes
            acc_ref[hf, mp] = (_lane_tile(alpha, dv // LANES) * acc_ref[hf, mp]
                               + jnp.dot(p.astype(BF16), v_blk, preferred_element_type=F32))
            m_ref[hf, mp] = m_new

    def kv_block(kb):
        ks = pl.multiple_of(kb * th, th)
        return k_ref[pl.ds(ks, th), :], v_ref[pl.ds(ks, th), :]

    def offset(n_blocks):
        return -slope2 * (n_blocks * th).astype(F32)

    def body(kb, carry):
        k_blk, v_blk = kv_block(kb)
        half_step(0, k_blk, v_blk, None, offset(2 * qi - kb))
        half_step(1, k_blk, v_blk, None, offset(2 * qi + 1 - kb))
        return carry

    lax.fori_loop(0, 2 * qi, body, 0)
    causal = (lax.broadcasted_iota(jnp.int32, (th, th), 0) >= lax.broadcasted_iota(jnp.int32, (th, th), 1))
    k_blk, v_blk = kv_block(2 * qi)
    half_step(0, k_blk, v_blk, causal, 0.0)
    half_step(1, k_blk, v_blk, None, -slope2 * float(th))
    k_blk, v_blk = kv_block(2 * qi + 1)
    half_step(1, k_blk, v_blk, causal, 0.0)

    lam = (jnp.exp(jnp.sum(lq1_ref[...] * lk1_ref[...], axis=1, keepdims=True))
           - jnp.exp(jnp.sum(lq2_ref[...] * lk2_ref[...], axis=1, keepdims=True)) + lam_init)
    for hf in range(2):
        inv = [1.0 / jnp.sum(l_ref[hf, mp], axis=1, keepdims=True) for mp in range(2)]
        o = acc_ref[hf, 0] * inv[0] - lam * (acc_ref[hf, 1] * inv[1])
        o = o * lax.rsqrt(jnp.mean(o * o, axis=-1, keepdims=True) + SUBLN_EPS) * g_ref[...]
        y_ref[hf * th:(hf + 1) * th, :] = (o * (1.0 - lam_init)).astype(y_ref.dtype)


def _diff_attn(a_qkv, slopes, subln_g, lq1, lk1, lq2, lk2, lam_init, casts, *, batch, seq, d_model, th=512):
    dh = d_model // 32
    dv = 2 * dh
    th = _tile(seq // 2, th)
    tq = 2 * th
    nq = seq // tq
    t = batch * seq
    k_blk0 = A_HEADS
    v_blk0 = 2 * A_HEADS
    vec = lambda n: pl.BlockSpec((1, n), lambda b, h, i: (0, 0))
    c_ops, c_in, c_out, c_shapes = _cast_side_stream(casts, batch * A_HEADS * nq,
                                                     lambda b, h, i: (b * A_HEADS + h) * nq + i)
    return pl.pallas_call(
        functools.partial(_attn_kernel, th=th, dh=dh, lam_init=lam_init, n_cast=len(casts)),
        grid=(batch, A_HEADS, nq),
        in_specs=[pl.BlockSpec(memory_space=pltpu.SMEM),
                  pl.BlockSpec((tq, dv), lambda b, h, i: (b * nq + i, h)),
                  pl.BlockSpec((seq, dv), lambda b, h, i: (b, k_blk0 + h)),
                  pl.BlockSpec((seq, dv), lambda b, h, i: (b, v_blk0 + h)),
                  vec(dv), vec(dh), vec(dh), vec(dh), vec(dh)] + c_in,
        out_specs=[pl.BlockSpec((tq, dv), lambda b, h, i: (b * nq + i, h))] + c_out,
        out_shape=[jax.ShapeDtypeStruct((t, A_HEADS * dv), BF16)] + c_shapes,
        scratch_shapes=[pltpu.VMEM((2, 2, th, LANES), F32), pltpu.VMEM((2, 2, th, LANES), F32),
                        pltpu.VMEM((2, 2, th, dv), F32), pltpu.VMEM((th, LANES), F32)],
        compiler_params=_params("arbitrary", "arbitrary", "arbitrary"),
        name="diff_attn",
    )(slopes, a_qkv, a_qkv, a_qkv, subln_g.reshape(1, dv),
      lq1.reshape(1, dh), lk1.reshape(1, dh), lq2.reshape(1, dh), lk2.reshape(1, dh), *c_ops)


def _conv_kernel(cb_ref, cc_ref, cx_ref, w_ref, y_ref):
    z = cc_ref[...] * cx_ref[...]
    rows = lax.broadcasted_iota(jnp.int32, z.shape, 0)
    w = w_ref[...]
    y = w[C_KW - 1:C_KW, :] * z
    for back in range(1, C_KW):
        zs = jnp.where(rows >= back, pltpu.roll(z, back, axis=0), 0.0)
        y = y + w[C_KW - 1 - back:C_KW - back, :] * zs
    y_ref[...] = (cb_ref[...] * y).astype(y_ref.dtype)


def _short_conv(c_bcx, conv_w, *, batch, seq, d_model, tc=256):
    cw = d_model // 4
    tc = _tile(cw, tc)
    per = cw // tc
    t = batch * seq
    return pl.pallas_call(
        _conv_kernel,
        grid=(batch, per),
        in_specs=[pl.BlockSpec((seq, tc), lambda b, j: (b, j)),
                  pl.BlockSpec((seq, tc), lambda b, j: (b, per + j)),
                  pl.BlockSpec((seq, tc), lambda b, j: (b, 2 * per + j)),
                  pl.BlockSpec((C_KW, tc), lambda b, j: (0, j))],
        out_specs=pl.BlockSpec((seq, tc), lambda b, j: (b, j)),
        out_shape=jax.ShapeDtypeStruct((t, cw), BF16),
        compiler_params=_params("parallel", "parallel"),
        name="short_conv",
    )(c_bcx, c_bcx, c_bcx, conv_w)


def kernel(x, w_in, m_b_i, m_b_f, a_lq1, a_lk1, a_lq2, a_lk2, a_subln_g, c_conv_w, w_out,
           ln1_g, ln1_b, w_up, w_down, ln2_g, ln2_b):
    batch, seq, d = x.shape
    depth = w_in.shape[0]
    t = batch * seq
    alpha = (2 * depth) ** 0.25
    chunk = _tile(seq, 256)
    nc = seq // chunk

    o_gate = 3 * d // 2
    o_aq = o_gate + N_GATES
    w_main_t = jnp.swapaxes(w_in, 1, 2).astype(BF16)
    w_tail_t = w_main_t[:, o_aq:, :]
    w_g_t = jnp.pad(w_main_t[:, o_gate:o_aq, :], ((0, 0), (0, LANES - N_GATES), (0, 0)))
    slopes = jnp.asarray([2.0 ** (-8.0 * (i + 1) / A_HEADS) for i in range(A_HEADS)], F32)

    resid = (x.reshape(t, d),)
    xb = resid[0].astype(BF16)
    for l in range(depth):
        m_qkv, gates = _matmul_nt(xb, w_main_t, l, BF16, name="in_proj_mqkv", n=d, side=w_g_t)
        m_o = _matmul_nt(xb, w_main_t, l, F32, name="in_proj_mo", n=d // 2, row0=d)
        a_qkv = _matmul_nt(xb, w_tail_t, l, BF16, name="in_proj_aqkv", n=3 * d // 4)
        c_bcx = _matmul_nt(xb, w_tail_t, l, F32, name="in_proj_conv", n=3 * d // 4, row0=3 * d // 4)
        gates = gates[:, :N_GATES].reshape(batch, nc, chunk, N_GATES).transpose(0, 1, 3, 2)
        bias = jnp.stack([m_b_i[l], m_b_f[l]])

        y_m = _mlstm(m_qkv, m_o, gates, bias, batch=batch, seq=seq, d_model=d, chunk=chunk)
        lam_init = 0.8 - 0.6 * math.exp(-0.3 * l)
        y_a, w_up_l, w_down_l, w_out_l = _diff_attn(
            a_qkv, slopes, a_subln_g[l], a_lq1[l], a_lk1[l], a_lq2[l], a_lk2[l], lam_init,
            [(w_up, l), (w_down, l), (w_out, l)], batch=batch, seq=seq, d_model=d)
        y_c = _short_conv(c_bcx, c_conv_w[l], batch=batch, seq=seq, d_model=d)
        r1 = _matmul([y_m, y_a, y_c], w_out_l, F32, name="out_proj", residual=resid, res_scale=alpha)
        xb, mu, rstd = _layer_norm(r1, ln1_g[l], ln1_b[l], keep_f32=False)
        resid = (r1, mu, rstd, ln1_g[l].reshape(1, d), ln1_b[l].reshape(1, d))
        hid = _matmul([xb], w_up_l, BF16, name="mlp_up", relu2=True)
        r2 = _matmul_ktiled(hid, w_down_l, resid, alpha, name="mlp_down")
        if l + 1 < depth:
            xb, mu, rstd = _layer_norm(r2, ln2_g[l], ln2_b[l], keep_f32=False)
            resid = (r2, mu, rstd, ln2_g[l].reshape(1, d), ln2_b[l].reshape(1, d))
    out, = _layer_norm(r2, ln2_g[depth - 1], ln2_b[depth - 1], keep_f32=True)
    return out.reshape(batch, seq, d)
```

```python
import functools
import math

import jax
import jax.numpy as jnp
from jax import lax
from jax.experimental import pallas as pl
from jax.experimental.pallas import tpu as pltpu

M_HEADS = 4
A_HEADS = 4
C_KW = 3
N_GATES = 2 * M_HEADS
LN_EPS = 1e-5
SUBLN_EPS = 1e-5

V7X_VMEM_LIMIT_BYTES = 56 * 1024 * 1024
LANES = 128
NEG_BIG = -1e30
LOG2E = 1.4426950408889634

F32 = jnp.float32
BF16 = jnp.bfloat16


def _tile(n, want):
    t = min(n, want)
    while n % t:
        t //= 2
    return t


def _params(*sem):
    return pltpu.CompilerParams(dimension_semantics=sem, vmem_limit_bytes=V7X_VMEM_LIMIT_BYTES)


def _lane_tile(x, reps):
    return x if reps == 1 else jnp.concatenate([x] * reps, axis=1)


def _mm_kernel(*refs, n_x, relu2, res_scale):
    x_refs, w_ref, o_ref = refs[:n_x], refs[n_x], refs[-1]
    acc = None
    k0 = 0
    for x_ref in x_refs:
        kp = x_ref.shape[1]
        part = jnp.dot(x_ref[...], w_ref[k0:k0 + kp, :], preferred_element_type=F32)
        acc = part if acc is None else acc + part
        k0 += kp
    if relu2:
        acc = jnp.square(jnp.maximum(acc, 0.0))
    if res_scale is not None:
        acc = res_scale * refs[n_x + 1][...] + acc
    o_ref[...] = acc.astype(o_ref.dtype)


def _matmul(xs, w, out_dtype, *, name, relu2=False, residual=None, res_scale=None, tm=1024, tn=1024):
    m = xs[0].shape[0]
    k, n = w.shape
    tm, tn = _tile(m, tm), _tile(n, tn)
    assert sum(xp.shape[1] for xp in xs) == k
    tile = pl.BlockSpec((tm, tn), lambda i, j: (i, j))
    extra = [] if residual is None else [residual]
    return pl.pallas_call(
        functools.partial(_mm_kernel, n_x=len(xs), relu2=relu2, res_scale=res_scale),
        grid=(m // tm, n // tn),
        in_specs=[pl.BlockSpec((tm, xp.shape[1]), lambda i, j: (i, 0)) for xp in xs]
        + [pl.BlockSpec((k, tn), lambda i, j: (0, j))] + [tile] * len(extra),
        out_specs=tile,
        out_shape=jax.ShapeDtypeStruct((m, n), out_dtype),
        compiler_params=_params("parallel", "parallel"),
        name=name,
    )(*xs, w, *extra)


_NT_DIMS = (((1,), (1,)), ((), ()))


def _mm_nt_kernel(x_ref, wt_ref, *rest):
    if len(rest) == 1:
        (o_ref,) = rest
    else:
        ws_ref, o_ref, os_ref = rest

        @pl.when(pl.program_id(1) == 0)
        def _():
            os_ref[...] = lax.dot_general(x_ref[...], ws_ref[...], _NT_DIMS, preferred_element_type=F32)

    o_ref[...] = lax.dot_general(x_ref[...], wt_ref[...], _NT_DIMS, preferred_element_type=F32).astype(o_ref.dtype)


def _matmul_nt(x, wt, layer, out_dtype, *, name, n, row0=0, side=None, tm=1024, tn=1024):
    m, k = x.shape
    tm, tn = _tile(m, tm), _tile(n, tn)
    assert row0 % tn == 0 and wt.shape[2] == k
    rb0 = row0 // tn
    operands = [x, wt]
    in_specs = [pl.BlockSpec((tm, k), lambda i, j: (i, 0)),
                pl.BlockSpec((None, tn, k), lambda i, j: (layer, rb0 + j, 0))]
    out_specs = pl.BlockSpec((tm, tn), lambda i, j: (i, j))
    out_shape = jax.ShapeDtypeStruct((m, n), out_dtype)
    if side is not None:
        r = side.shape[1]
        operands.append(side)
        in_specs.append(pl.BlockSpec((None, r, k), lambda i, j: (layer, 0, 0)))
        out_specs = [out_specs, pl.BlockSpec((tm, r), lambda i, j: (i, 0))]
        out_shape = [out_shape, jax.ShapeDtypeStruct((m, r), F32)]
    return pl.pallas_call(
        _mm_nt_kernel,
        grid=(m // tm, n // tn),
        in_specs=in_specs,
        out_specs=out_specs,
        out_shape=out_shape,
        compiler_params=_params("parallel", "arbitrary"),
        name=name,
    )(*operands)


def _mm_acc_kernel(x_ref, w_ref, res_ref, o_ref, *, res_scale):
    @pl.when(pl.program_id(2) == 0)
    def _():
        o_ref[...] = res_scale * res_ref[...]

    o_ref[...] += jnp.dot(x_ref[...], w_ref[...], preferred_element_type=F32)


def _matmul_ktiled(x, w, residual, res_scale, *, name, tm=1024, tn=1024, tk=4096):
    m, k = x.shape
    n = w.shape[1]
    tm, tn, tk = _tile(m, tm), _tile(n, tn), _tile(k, tk)
    tile = pl.BlockSpec((tm, tn), lambda i, j, kk: (i, j))
    return pl.pallas_call(
        functools.partial(_mm_acc_kernel, res_scale=res_scale),
        grid=(m // tm, n // tn, k // tk),
        in_specs=[pl.BlockSpec((tm, tk), lambda i, j, kk: (i, kk)),
                  pl.BlockSpec((tk, tn), lambda i, j, kk: (kk, j)), tile],
        out_specs=tile,
        out_shape=jax.ShapeDtypeStruct((m, n), F32),
        compiler_params=_params("parallel", "parallel", "arbitrary"),
        name=name,
    )(x, w, residual)


def _ln_kernel(r_ref, g_ref, b_ref, o_ref, ob_ref):
    r = r_ref[...]
    mu = jnp.mean(r, axis=-1, keepdims=True)
    d = r - mu
    var = jnp.mean(d * d, axis=-1, keepdims=True)
    o = d * lax.rsqrt(var + LN_EPS) * g_ref[...] + b_ref[...]
    o_ref[...] = o
    ob_ref[...] = o.astype(BF16)


def _layer_norm(r, g, b, *, tr=256):
    t, d = r.shape
    tr = _tile(t, tr)
    row = pl.BlockSpec((tr, d), lambda i: (i, 0))
    vec = pl.BlockSpec((1, d), lambda i: (0, 0))
    return pl.pallas_call(
        _ln_kernel,
        grid=(t // tr,),
        in_specs=[row, vec, vec],
        out_specs=[row, row],
        out_shape=[jax.ShapeDtypeStruct((t, d), F32), jax.ShapeDtypeStruct((t, d), BF16)],
        compiler_params=_params("parallel"),
        name="layer_norm",
    )(r, g.reshape(1, d), b.reshape(1, d))


def _cast_side_stream(casts, n_steps, step_of):
    operands, in_specs, out_specs, out_shapes = [], [], [], []
    for w, layer in casts:
        _, r, c = w.shape
        rows = r // n_steps
        assert rows * n_steps == r and rows % 16 == 0
        operands.append(w)
        in_specs.append(pl.BlockSpec((None, rows, c), lambda *g, layer=layer: (layer, step_of(*g), 0)))
        out_specs.append(pl.BlockSpec((rows, c), lambda *g: (step_of(*g), 0)))
        out_shapes.append(jax.ShapeDtypeStruct((r, c), BF16))
    return operands, in_specs, out_specs, out_shapes


def _round_weights(src_refs, dst_refs):
    for src, dst in zip(src_refs, dst_refs, strict=True):
        dst[...] = src[...].astype(dst.dtype)


def _mlstm_kernel(bias_ref, q_ref, k_ref, v_ref, og_ref, gate_ref, y_ref, c_ref, n_ref, m_ref, *, dqk, dv):
    @pl.when(pl.program_id(1) == 0)
    def _():
        c_ref[...] = jnp.zeros_like(c_ref)
        n_ref[...] = jnp.zeros_like(n_ref)
        m_ref[...] = jnp.zeros_like(m_ref)

    ln = q_ref.shape[0]
    scale = float(dqk) ** -0.5
    rows = lax.broadcasted_iota(jnp.int32, (ln, ln), 0)
    cols = lax.broadcasted_iota(jnp.int32, (ln, ln), 1)
    causal = cols <= rows
    eye = cols == rows

    def to_col(row):
        return jnp.sum(jnp.where(eye, jnp.broadcast_to(row, (ln, ln)), 0.0), axis=1, keepdims=True)

    def to_row(col):
        return jnp.sum(jnp.where(eye, jnp.broadcast_to(col, (ln, ln)), 0.0), axis=0, keepdims=True)

    for head in range(M_HEADS):
        q = q_ref[:, head * dqk:(head + 1) * dqk]
        k = k_ref[:, head * dqk:(head + 1) * dqk]
        v = v_ref[:, head * dv:(head + 1) * dv]
        i_row = gate_ref[head:head + 1, :] + bias_ref[0, head]
        f_row = gate_ref[M_HEADS + head:M_HEADS + head + 1, :] + bias_ref[1, head]
        lf_row = jnp.minimum(f_row, 0.0) - jnp.log(1.0 + jnp.exp(-jnp.abs(f_row)))

        b_col = jnp.sum(jnp.where(causal, jnp.broadcast_to(lf_row, (ln, ln)), 0.0), axis=1, keepdims=True)
        b_row = to_row(b_col)

        m_prev = m_ref[head]
        dmat = jnp.where(causal, b_col - b_row + i_row, -jnp.inf)
        inter = b_col + m_prev
        m_row = jnp.maximum(inter, jnp.max(dmat, axis=1, keepdims=True))
        qk = lax.dot_general(q, k, _NT_DIMS, preferred_element_type=F32) * scale
        w = qk * jnp.exp(dmat - m_row)
        g = jnp.exp(inter - m_row)

        c_old = c_ref[head]
        n_old = n_ref[head]
        q_c = jnp.dot(q, c_old.astype(BF16), preferred_element_type=F32)
        num = jnp.dot(w.astype(BF16), v, preferred_element_type=F32) + g * q_c
        den = (jnp.sum(w, axis=1, keepdims=True)
               + g * jnp.sum(q.astype(F32) * n_old, axis=1, keepdims=True))
        hid = num * (1.0 / jnp.maximum(jnp.abs(den), jnp.exp(-m_row)))
        gate = 1.0 / (1.0 + jnp.exp(-og_ref[:, head * dv:(head + 1) * dv]))
        y_ref[:, head * dv:(head + 1) * dv] = (gate * hid).astype(y_ref.dtype)

        m_new = m_row[ln - 1:ln, :]
        b_last = b_col[ln - 1:ln, :]
        wl_row = jnp.exp(b_last - b_row + i_row - m_new)
        gl = jnp.exp(b_last + m_prev - m_new)
        wl_col = to_col(wl_row) * scale
        v_w = (wl_col * v.astype(F32)).astype(BF16)
        kv = lax.dot_general(k, v_w, (((0,), (0,)), ((), ())), preferred_element_type=F32)
        c_ref[head] = gl * c_old + kv
        n_ref[head] = gl * n_old + jnp.sum(wl_col * k.astype(F32), axis=0, keepdims=True)
        m_ref[head] = m_new


def _mlstm(m_qkv, m_o, gates, bias, *, batch, seq, d_model, chunk):
    dqk, dv = d_model // 16, d_model // 8
    qw, vw = M_HEADS * dqk, M_HEADS * dv
    nc = seq // chunk
    t = batch * seq
    return pl.pallas_call(
        functools.partial(_mlstm_kernel, dqk=dqk, dv=dv),
        grid=(batch, nc),
        in_specs=[pl.BlockSpec(memory_space=pltpu.SMEM),
                  pl.BlockSpec((chunk, qw), lambda b, c: (b * nc + c, 0)),
                  pl.BlockSpec((chunk, qw), lambda b, c: (b * nc + c, 1)),
                  pl.BlockSpec((chunk, vw), lambda b, c: (b * nc + c, 1)),
                  pl.BlockSpec((chunk, vw), lambda b, c: (b * nc + c, 0)),
                  pl.BlockSpec((None, None, N_GATES, chunk), lambda b, c: (b, c, 0, 0))],
        out_specs=pl.BlockSpec((chunk, vw), lambda b, c: (b * nc + c, 0)),
        out_shape=jax.ShapeDtypeStruct((t, vw), BF16),
        scratch_shapes=[pltpu.VMEM((M_HEADS, dqk, dv), F32), pltpu.VMEM((M_HEADS, 1, dqk), F32),
                        pltpu.VMEM((M_HEADS, 1, 1), F32)],
        compiler_params=_params("parallel", "arbitrary"),
        name="mlstm",
    )(bias, m_qkv, m_qkv, m_qkv, m_o, gates)


def _attn_kernel(*refs, th, dh, lam_init, n_cast):
    slope_ref, q_ref, k_ref, v_ref, g_ref, lq1_ref, lk1_ref, lq2_ref, lk2_ref = refs[:9]
    y_ref = refs[9 + n_cast]
    m_ref, l_ref, acc_ref, rowt_ref = refs[10 + 2 * n_cast:]
    _round_weights(refs[9:9 + n_cast], refs[10 + n_cast:10 + 2 * n_cast])

    head = pl.program_id(1)
    qi = pl.program_id(2)
    dv = acc_ref.shape[-1]
    slope2 = slope_ref[head] * LOG2E
    cs = (float(dh) ** -0.5) * LOG2E

    m_ref[...] = jnp.full_like(m_ref, NEG_BIG)
    l_ref[...] = jnp.zeros_like(l_ref)
    acc_ref[...] = jnp.zeros_like(acc_ref)
    col_bias = slope2 * lax.broadcasted_iota(jnp.int32, (1, th), 1).astype(F32)
    rowt_ref[...] = slope2 * lax.broadcasted_iota(jnp.int32, (th, LANES), 0).astype(F32)

    def half_step(hf, k_blk, v_blk, causal, c0):
        q = q_ref[hf * th:(hf + 1) * th, :]
        row_off = c0 - rowt_ref[...]
        for mp in range(2):
            t2 = lax.dot_general(q[:, mp * dh:(mp + 1) * dh], k_blk[:, mp * dh:(mp + 1) * dh],
                                 _NT_DIMS, preferred_element_type=F32) * cs + col_bias
            if causal is not None:
                t2 = jnp.where(causal, t2, NEG_BIG)
            m_old = m_ref[hf, mp]
            m_new = jnp.maximum(m_old, jnp.max(t2, axis=1, keepdims=True) + row_off)
            p = jnp.exp2(t2 - _lane_tile(m_new - row_off, th // LANES))
            alpha = jnp.exp2(m_old - m_new)
            p_lanes = p[:, :LANES]
            for c in range(1, th // LANES):
                p_lanes = p_lanes + p[:, c * LANES:(c + 1) * LANES]
            l_ref[hf, mp] = alpha * l_ref[hf, mp] + p_lanes
            acc_ref[hf, mp] = (_lane_tile(alpha, dv // LANES) * acc_ref[hf, mp]
                               + jnp.dot(p.astype(BF16), v_blk, preferred_element_type=F32))
            m_ref[hf, mp] = m_new

    def kv_block(kb):
        ks = pl.multiple_of(kb * th, th)
        return k_ref[pl.ds(ks, th), :], v_ref[pl.ds(ks, th), :]

    def offset(n_blocks):
        return -slope2 * (n_blocks * th).astype(F32)

    def body(kb, carry):
        k_blk, v_blk = kv_block(kb)
        half_step(0, k_blk, v_blk, None, offset(2 * qi - kb))
        half_step(1, k_blk, v_blk, None, offset(2 * qi + 1 - kb))
        return carry

    lax.fori_loop(0, 2 * qi, body, 0)
    causal = (lax.broadcasted_iota(jnp.int32, (th, th), 0) >= lax.broadcasted_iota(jnp.int32, (th, th), 1))
    k_blk, v_blk = kv_block(2 * qi)
    half_step(0, k_blk, v_blk, causal, 0.0)
    half_step(1, k_blk, v_blk, None, -slope2 * float(th))
    k_blk, v_blk = kv_block(2 * qi + 1)
    half_step(1, k_blk, v_blk, causal, 0.0)

    lam = (jnp.exp(jnp.sum(lq1_ref[...] * lk1_ref[...], axis=1, keepdims=True))
           - jnp.exp(jnp.sum(lq2_ref[...] * lk2_ref[...], axis=1, keepdims=True)) + lam_init)
    for hf in range(2):
        inv = [1.0 / jnp.sum(l_ref[hf, mp], axis=1, keepdims=True) for mp in range(2)]
        o = acc_ref[hf, 0] * inv[0] - lam * (acc_ref[hf, 1] * inv[1])
        o = o * lax.rsqrt(jnp.mean(o * o, axis=-1, keepdims=True) + SUBLN_EPS) * g_ref[...]
        y_ref[hf * th:(hf + 1) * th, :] = (o * (1.0 - lam_init)).astype(y_ref.dtype)


def _diff_attn(a_qkv, slopes, subln_g, lq1, lk1, lq2, lk2, lam_init, casts, *, batch, seq, d_model, th=512):
    dh = d_model // 32
    dv = 2 * dh
    th = _tile(seq // 2, th)
    tq = 2 * th
    nq = seq // tq
    t = batch * seq
    k_blk0 = A_HEADS
    v_blk0 = 2 * A_HEADS
    vec = lambda n: pl.BlockSpec((1, n), lambda b, h, i: (0, 0))
    c_ops, c_in, c_out, c_shapes = _cast_side_stream(casts, batch * A_HEADS * nq,
                                                     lambda b, h, i: (b * A_HEADS + h) * nq + i)
    return pl.pallas_call(
        functools.partial(_attn_kernel, th=th, dh=dh, lam_init=lam_init, n_cast=len(casts)),
        grid=(batch, A_HEADS, nq),
        in_specs=[pl.BlockSpec(memory_space=pltpu.SMEM),
                  pl.BlockSpec((tq, dv), lambda b, h, i: (b * nq + i, h)),
                  pl.BlockSpec((seq, dv), lambda b, h, i: (b, k_blk0 + h)),
                  pl.BlockSpec((seq, dv), lambda b, h, i: (b, v_blk0 + h)),
                  vec(dv), vec(dh), vec(dh), vec(dh), vec(dh)] + c_in,
        out_specs=[pl.BlockSpec((tq, dv), lambda b, h, i: (b * nq + i, h))] + c_out,
        out_shape=[jax.ShapeDtypeStruct((t, A_HEADS * dv), BF16)] + c_shapes,
        scratch_shapes=[pltpu.VMEM((2, 2, th, LANES), F32), pltpu.VMEM((2, 2, th, LANES), F32),
                        pltpu.VMEM((2, 2, th, dv), F32), pltpu.VMEM((th, LANES), F32)],
        compiler_params=_params("arbitrary", "arbitrary", "arbitrary"),
        name="diff_attn",
    )(slopes, a_qkv, a_qkv, a_qkv, subln_g.reshape(1, dv),
      lq1.reshape(1, dh), lk1.reshape(1, dh), lq2.reshape(1, dh), lk2.reshape(1, dh), *c_ops)


def _conv_kernel(cb_ref, cc_ref, cx_ref, w_ref, y_ref):
    z = cc_ref[...] * cx_ref[...]
    rows = lax.broadcasted_iota(jnp.int32, z.shape, 0)
    w = w_ref[...]
    y = w[C_KW - 1:C_KW, :] * z
    for back in range(1, C_KW):
        zs = jnp.where(rows >= back, pltpu.roll(z, back, axis=0), 0.0)
        y = y + w[C_KW - 1 - back:C_KW - back, :] * zs
    y_ref[...] = (cb_ref[...] * y).astype(y_ref.dtype)


def _short_conv(c_bcx, conv_w, *, batch, seq, d_model, tc=256):
    cw = d_model // 4
    tc = _tile(cw, tc)
    per = cw // tc
    t = batch * seq
    return pl.pallas_call(
        _conv_kernel,
        grid=(batch, per),
        in_specs=[pl.BlockSpec((seq, tc), lambda b, j: (b, j)),
                  pl.BlockSpec((seq, tc), lambda b, j: (b, per + j)),
                  pl.BlockSpec((seq, tc), lambda b, j: (b, 2 * per + j)),
                  pl.BlockSpec((C_KW, tc), lambda b, j: (0, j))],
        out_specs=pl.BlockSpec((seq, tc), lambda b, j: (b, j)),
        out_shape=jax.ShapeDtypeStruct((t, cw), BF16),
        compiler_params=_params("parallel", "parallel"),
        name="short_conv",
    )(c_bcx, c_bcx, c_bcx, conv_w)


def kernel(x, w_in, m_b_i, m_b_f, a_lq1, a_lk1, a_lq2, a_lk2, a_subln_g, c_conv_w, w_out,
           ln1_g, ln1_b, w_up, w_down, ln2_g, ln2_b):
    batch, seq, d = x.shape
    depth = w_in.shape[0]
    t = batch * seq
    alpha = (2 * depth) ** 0.25
    chunk = _tile(seq, 256)
    nc = seq // chunk

    o_gate = 3 * d // 2
    o_aq = o_gate + N_GATES
    w_main_t = jnp.swapaxes(w_in, 1, 2).astype(BF16)
    w_tail_t = w_main_t[:, o_aq:, :]
    w_g_t = jnp.pad(w_main_t[:, o_gate:o_aq, :], ((0, 0), (0, LANES - N_GATES), (0, 0)))
    slopes = jnp.asarray([2.0 ** (-8.0 * (i + 1) / A_HEADS) for i in range(A_HEADS)], F32)

    xf = x.reshape(t, d)
    xb = xf.astype(BF16)
    for l in range(depth):
        m_qkv, gates = _matmul_nt(xb, w_main_t, l, BF16, name="in_proj_mqkv", n=d, side=w_g_t)
        m_o = _matmul_nt(xb, w_main_t, l, F32, name="in_proj_mo", n=d // 2, row0=d)
        a_qkv = _matmul_nt(xb, w_tail_t, l, BF16, name="in_proj_aqkv", n=3 * d // 4)
        c_bcx = _matmul_nt(xb, w_tail_t, l, F32, name="in_proj_conv", n=3 * d // 4, row0=3 * d // 4)
        gates = gates[:, :N_GATES].reshape(batch, nc, chunk, N_GATES).transpose(0, 1, 3, 2)
        bias = jnp.stack([m_b_i[l], m_b_f[l]])

        y_m = _mlstm(m_qkv, m_o, gates, bias, batch=batch, seq=seq, d_model=d, chunk=chunk)
        lam_init = 0.8 - 0.6 * math.exp(-0.3 * l)
        y_a, w_up_l, w_down_l, w_out_l = _diff_attn(
            a_qkv, slopes, a_subln_g[l], a_lq1[l], a_lk1[l], a_lq2[l], a_lk2[l], lam_init,
            [(w_up, l), (w_down, l), (w_out, l)], batch=batch, seq=seq, d_model=d)
        y_c = _short_conv(c_bcx, c_conv_w[l], batch=batch, seq=seq, d_model=d)
        r1 = _matmul([y_m, y_a, y_c], w_out_l, F32, name="out_proj", residual=xf, res_scale=alpha)
        xf, xb = _layer_norm(r1, ln1_g[l], ln1_b[l])
        hid = _matmul([xb], w_up_l, BF16, name="mlp_up", relu2=True)
        r2 = _matmul_ktiled(hid, w_down_l, xf, alpha, name="mlp_down")
        xf, xb = _layer_norm(r2, ln2_g[l], ln2_b[l])
    return xf.reshape(batch, seq, d)
```
